```python
import math
import jax, jax.numpy as jnp
from jax import lax
import numpy as np

D_MODEL = 1024
BATCH = 8
SEQ = 8192
DEPTH = 1

D_MIX = D_MODEL
ATT_WIDTH = D_MIX // 2
SSM_WIDTH = D_MIX - ATT_WIDTH
ATT_HEADS = 4
ATT_QK_DIM = 64
ATT_V_DIM = ATT_WIDTH // ATT_HEADS
QK_COLS = ATT_HEADS * 2 * ATT_QK_DIM
SSM_GROUP = 16
SSM_GROUPS = SSM_WIDTH // SSM_GROUP
SSM_STATE = 64
D_FF = int(math.ceil(8 * D_MODEL / 3 / 256)) * 256
REL_BUCKETS = 32
REL_MAX_DIST = 128
Q_BLOCK = 128
EPS = 1e-6
IN_COLS = 2 * QK_COLS + ATT_WIDTH + SSM_WIDTH

kernel_name = "hybrid_diffattn_s5_parallel_heads"


def rms_norm(x, g):
    xf = x.astype(jnp.float32)
    y = xf * lax.rsqrt(jnp.mean(xf * xf, axis=-1, keepdims=True) + EPS)
    return (y * g.astype(jnp.float32)).astype(x.dtype)


def t5_bucket(n):
    max_exact = REL_BUCKETS // 2
    nf = jnp.maximum(n, 1).astype(jnp.float32)
    large = max_exact + (jnp.log(nf / max_exact) / math.log(REL_MAX_DIST / max_exact)
                         * (REL_BUCKETS - max_exact)).astype(jnp.int32)
    large = jnp.minimum(large, REL_BUCKETS - 1)
    return jnp.where(n < max_exact, n, large)


def diff_attention(q, k, v, lam, rel_bias):
    bsz, seq = q.shape[0], q.shape[1]
    nblk = seq // Q_BLOCK
    scale = ATT_QK_DIM ** -0.5
    k_pos = jnp.arange(seq)
    table = rel_bias.astype(jnp.float32)

    def block(i):
        q0 = i * Q_BLOCK
        qb = lax.dynamic_slice_in_dim(q, q0, Q_BLOCK, axis=1)
        logits = jnp.einsum('bqhcd,bkhcd->bhcqk', qb, k).astype(jnp.float32) * scale
        dist = (q0 + jnp.arange(Q_BLOCK))[:, None] - k_pos[None, :]
        bias = jnp.transpose(table[t5_bucket(jnp.maximum(dist, 0))], (2, 0, 1))
        logits = logits + bias[None, :, None]
        logits = jnp.where((dist >= 0)[None, None, None], logits, -jnp.inf)
        p = jax.nn.softmax(logits, axis=-1)
        w = p[:, :, 0] - lam * p[:, :, 1]
        return jnp.einsum('bhqk,bkhd->bqhd', w.astype(v.dtype), v)

    out = lax.map(block, jnp.arange(nblk))
    return jnp.moveaxis(out, 0, 1).reshape(bsz, seq, ATT_HEADS, ATT_V_DIM)


def s5_ssm(u, A_re, A_im, log_dt, B_re, B_im, C_re, C_im, D_skip):
    bsz, seq, _ = u.shape
    f32 = jnp.float32
    ug = u.astype(f32).reshape(bsz, seq, SSM_GROUPS, SSM_GROUP)
    lam = lax.complex(A_re.astype(f32), A_im.astype(f32))
    dt = jnp.exp(log_dt.astype(f32))[:, None]
    lam_bar = jnp.exp(lam * dt)
    b_bar = ((lam_bar - 1.0) / lam)[:, :, None] * lax.complex(B_re.astype(f32), B_im.astype(f32))
    bu = jnp.einsum('gnc,bsgc->bsgn', b_bar, ug.astype(jnp.complex64))
    a = jnp.broadcast_to(lam_bar, bu.shape)

    def combine(left, right):
        a1, b1 = left
        a2, b2 = right
        return a2 * a1, a2 * b1 + b2

    _, xs = lax.associative_scan(combine, (a, bu), axis=1)
    c = lax.complex(C_re.astype(f32), C_im.astype(f32))
    y = jnp.einsum('gcn,bsgn->bsgc', c, xs).real + D_skip.astype(f32) * ug
    return y.reshape(bsz, seq, SSM_WIDTH).astype(u.dtype)


def setup_inputs(seed: int = 0) -> dict:
    key = jax.random.key(seed)
    ks = jax.random.split(key, 32)
    f32 = jnp.float32
    L = DEPTH

    def nrm(k, shape, scale):
        return jax.random.normal(k, shape, f32) * scale

    def gain(k, shape):
        return 1.0 + 0.05 * jax.random.normal(k, shape, f32)

    n_idx = jnp.arange(SSM_STATE, dtype=f32)
    A_re = -0.5 + 0.01 * jax.random.normal(ks[9], (L, SSM_GROUPS, SSM_STATE), f32)
    A_im = math.pi * n_idx[None, None, :] + 0.01 * jax.random.normal(ks[10], (L, SSM_GROUPS, SSM_STATE), f32)
    log_dt = jax.random.uniform(ks[11], (L, SSM_GROUPS), f32, math.log(1e-3), math.log(1e-1))
    return {
        "x": jax.random.normal(ks[0], (BATCH, SEQ, D_MODEL), f32),
        "norm_mix_g": gain(ks[1], (L, D_MODEL)),
        "w_in": nrm(ks[2], (L, D_MODEL, IN_COLS), D_MODEL ** -0.5),
        "lambda_q1": nrm(ks[3], (L, ATT_QK_DIM), 0.1),
        "lambda_k1": nrm(ks[4], (L, ATT_QK_DIM), 0.1),
        "lambda_q2": nrm(ks[5], (L, ATT_QK_DIM), 0.1),
        "lambda_k2": nrm(ks[6], (L, ATT_QK_DIM), 0.1),
        "subln_g": gain(ks[7], (L, ATT_V_DIM)),
        "rel_bias": nrm(ks[8], (REL_BUCKETS, ATT_HEADS), 0.5),
        "A_re": A_re,
        "A_im": A_im,
        "log_dt": log_dt,
        "B_re": nrm(ks[12], (L, SSM_GROUPS, SSM_STATE, SSM_GROUP), (2 * SSM_GROUP) ** -0.5),
        "B_im": nrm(ks[13], (L, SSM_GROUPS, SSM_STATE, SSM_GROUP), (2 * SSM_GROUP) ** -0.5),
        "C_re": nrm(ks[14], (L, SSM_GROUPS, SSM_GROUP, SSM_STATE), (2 * SSM_STATE) ** -0.5),
        "C_im": nrm(ks[15], (L, SSM_GROUPS, SSM_GROUP, SSM_STATE), (2 * SSM_STATE) ** -0.5),
        "D_skip": nrm(ks[16], (L, SSM_GROUPS, SSM_GROUP), 1.0),
        "w_glu": nrm(ks[17], (L, SSM_WIDTH, SSM_WIDTH), SSM_WIDTH ** -0.5),
        "b_glu": nrm(ks[18], (L, SSM_WIDTH), 0.01),
        "ssm_norm_g": gain(ks[19], (L, SSM_WIDTH)),
        "w_out": nrm(ks[20], (L, D_MIX, D_MODEL), D_MIX ** -0.5),
        "norm_ffn_g": gain(ks[21], (L, D_MODEL)),
        "w_gate": nrm(ks[22], (L, D_MODEL, D_FF), D_MODEL ** -0.5),
        "w_up": nrm(ks[23], (L, D_MODEL, D_FF), D_MODEL ** -0.5),
        "w_down": nrm(ks[24], (L, D_FF, D_MODEL), D_FF ** -0.5),
        "norm_final_g": gain(ks[25], (D_MODEL,)),
    }


def reference(x, norm_mix_g, w_in, lambda_q1, lambda_k1, lambda_q2, lambda_k2, subln_g,
              rel_bias, A_re, A_im, log_dt, B_re, B_im, C_re, C_im, D_skip, w_glu, b_glu,
              ssm_norm_g, w_out, norm_ffn_g, w_gate, w_up, w_down, norm_final_g):
    bsz, seq, _ = x.shape
    f32 = jnp.float32
    for l in range(DEPTH):
        h = rms_norm(x, norm_mix_g[l])
        proj = h @ w_in[l]
        q, k, v, u = jnp.split(proj, [QK_COLS, 2 * QK_COLS, 2 * QK_COLS + ATT_WIDTH], axis=-1)
        q = q.reshape(bsz, seq, ATT_HEADS, 2, ATT_QK_DIM)
        k = k.reshape(bsz, seq, ATT_HEADS, 2, ATT_QK_DIM)
        v = v.reshape(bsz, seq, ATT_HEADS, ATT_V_DIM)

        lam_init = 0.8 - 0.6 * math.exp(-0.3 * l)
        lam = (jnp.exp(jnp.sum(lambda_q1[l].astype(f32) * lambda_k1[l].astype(f32)))
               - jnp.exp(jnp.sum(lambda_q2[l].astype(f32) * lambda_k2[l].astype(f32)))
               + lam_init)
        att = diff_attention(q, k, v, lam, rel_bias)
        att = (rms_norm(att, subln_g[l]) * (1.0 - lam_init)).reshape(bsz, seq, ATT_WIDTH)

        y = s5_ssm(u, A_re[l], A_im[l], log_dt[l], B_re[l], B_im[l], C_re[l], C_im[l], D_skip[l])
        g = jax.nn.gelu(y)
        ssm = g * jax.nn.sigmoid(g @ w_glu[l] + b_glu[l])
        ssm = rms_norm(ssm, ssm_norm_g[l])

        x = x + jnp.concatenate([att, ssm], axis=-1) @ w_out[l]

        h = rms_norm(x, norm_ffn_g[l])
        x = x + (jax.nn.silu(h @ w_gate[l]) * (h @ w_up[l])) @ w_down[l]
    return rms_norm(x, norm_final_g)
```

```python
import functools
import math

import jax
import jax.numpy as jnp
import numpy as np
from jax import lax
from jax.experimental import pallas as pl
from jax.experimental.pallas import tpu as pltpu

F32 = jnp.float32
BF16 = jnp.bfloat16

D_MODEL = 1024
ATT_WIDTH = 512
SSM_WIDTH = 512
ATT_HEADS = 4
ATT_QK_DIM = 64
ATT_V_DIM = 128
QK_COLS = 512
SSM_GROUP = 16
SSM_GROUPS = 32
SSM_STATE = 64
REL_BUCKETS = 32
REL_MAX_DIST = 128
EPS = 1e-6
LOG2E = math.log2(math.e)
MASK_VALUE = -1e30

V7X_SUBLANES = 8
V7X_LANES = 128
V7X_VMEM_LIMIT_BYTES = 56 * 1024 * 1024

PROJ_ROWS = 512
ATT_TQ = 512
ATT_TK = 256
SSM_STEPS = 64
SSM_LANE_GROUPS = 4
FFN_ROWS = 512

_NT = (((1,), (1,)), ((), ()))


def _dot(a, b):
    return jnp.dot(a, b, preferred_element_type=F32)


def _rms(x, g):
    ms = jnp.mean(x * x, axis=-1, keepdims=True)
    return x * lax.rsqrt(ms + EPS) * g


def _inproj_kernel(x_ref, g_ref, wqT_ref, wk_ref, wvT_ref, wu_ref,
                   qT_ref, k_ref, vT_ref, u_ref, *, tk):
    h = _rms(x_ref[0], g_ref[...]).astype(BF16)
    qT = lax.dot_general(wqT_ref[...], h, _NT, preferred_element_type=F32)
    qT_ref[0] = qT.astype(BF16)
    k_ref[0] = _dot(h, wk_ref[...]).astype(BF16)
    vT = lax.dot_general(wvT_ref[...], h, _NT, preferred_element_type=F32).astype(BF16)
    for c in range(vT_ref.shape[1]):
        vT_ref[0, c] = vT[:, c * tk:(c + 1) * tk]
    u_ref[...] = _dot(h, wu_ref[...])


def _inproj(x, g, wqT, wk, wvT, wu, *, ts, tk):
    bsz, seq, d = x.shape
    ns = seq // ts
    const = lambda shape: pl.BlockSpec(shape, lambda b, s: (0,) * len(shape),
                                       pipeline_mode=pl.Buffered(1))
    return pl.pallas_call(
        functools.partial(_inproj_kernel, tk=tk),
        grid=(bsz, ns),
        in_specs=[
            pl.BlockSpec((1, ts, d), lambda b, s: (b, s, 0)),
            const((1, d)),
            const((QK_COLS, d)),
            const((d, QK_COLS)),
            const((ATT_WIDTH, d)),
            const((d, SSM_WIDTH)),
        ],
        out_specs=[
            pl.BlockSpec((1, QK_COLS, ts), lambda b, s: (b, 0, s)),
            pl.BlockSpec((1, ts, QK_COLS), lambda b, s: (b, s, 0)),
            pl.BlockSpec((1, ts // tk, ATT_WIDTH, tk), lambda b, s: (b, s, 0, 0)),
            pl.BlockSpec((ts, SSM_WIDTH), lambda b, s: (s, b)),
        ],
        out_shape=[
            jax.ShapeDtypeStruct((bsz, QK_COLS, seq), BF16),
            jax.ShapeDtypeStruct((bsz, seq, QK_COLS), BF16),
            jax.ShapeDtypeStruct((bsz, seq // tk, ATT_WIDTH, tk), BF16),
            jax.ShapeDtypeStruct((seq, bsz * SSM_WIDTH), F32),
        ],
        compiler_params=pltpu.CompilerParams(
            dimension_semantics=("arbitrary", "arbitrary"),
            vmem_limit_bytes=V7X_VMEM_LIMIT_BYTES),
        name="inproj",
    )(x, g, wqT, wk, wvT, wu)


def _attn_kernel(lam_ref, qT_ref, k_ref, vT_ref, bias_ref, g_ref, o_ref,
                 qbd, m_s, l_s, acc_s, *, tq, tk, out_scale):
    qi = pl.program_id(2)
    ratio = tq // tk
    dk = ATT_QK_DIM

    qbd[...] = jnp.zeros_like(qbd)
    qbd[0:dk, 0:tq] = qT_ref[0, 0:dk, :]
    qbd[dk:2 * dk, tq:2 * tq] = qT_ref[0, dk:2 * dk, :]
    m_s[...] = jnp.full_like(m_s, -jnp.inf)
    l_s[...] = jnp.zeros_like(l_s)
    acc_s[...] = jnp.zeros_like(acc_s)

    def tile(j, bias):
        r0 = pl.multiple_of(j * tk, tk)
        s = _dot(k_ref[0, pl.ds(r0, tk), :], qbd[...])
        if bias is not None:
            s = s + jnp.concatenate([bias, bias], axis=1)
        m_old = m_s[...]
        m_new = jnp.maximum(m_old, jnp.max(s, axis=0, keepdims=True))
        alpha = jnp.exp2(m_old - m_new)
        p = jnp.exp2(s - m_new)
        l_s[...] = alpha * l_s[...] + jnp.sum(p, axis=0, keepdims=True)
        acc_s[...] = alpha * acc_s[...] + _dot(vT_ref[0, j], p.astype(BF16))
        m_s[...] = m_new

    def far_body(j, carry):
        tile(j, None)
        return carry

    lax.fori_loop(0, jnp.maximum(qi * ratio - 1, 0), far_body, 0)

    @pl.when(qi > 0)
    def _():
        tile(qi * ratio - 1, bias_ref[0, 0])

    for r in range(1, ratio + 1):
        tile(qi * ratio + (r - 1), bias_ref[0, r])

    lam = lam_ref[0, 0]
    acc = acc_s[...]
    l = l_s[...]
    o = acc[:, 0:tq] / l[:, 0:tq] - lam * (acc[:, tq:2 * tq] / l[:, tq:2 * tq])
    ms = jnp.mean(o * o, axis=0, keepdims=True)
    o = o * lax.rsqrt(ms + EPS) * g_ref[...] * out_scale
    o_ref[0] = o.T.astype(BF16)


def _attention(lam, qT, k, vT, bias, g, *, tq, tk, out_scale):
    bsz, seq, _ = k.shape
    nq = seq // tq
    ratio = tq // tk
    return pl.pallas_call(
        functools.partial(_attn_kernel, tq=tq, tk=tk, out_scale=out_scale),
        grid=(bsz, ATT_HEADS, nq),
        in_specs=[
            pl.BlockSpec(memory_space=pltpu.SMEM),
            pl.BlockSpec((1, 2 * ATT_QK_DIM, tq), lambda b, h, q: (b, h, q)),
            pl.BlockSpec((1, seq, 2 * ATT_QK_DIM), lambda b, h, q: (b, 0, h)),
            pl.BlockSpec((1, seq // tk, ATT_V_DIM, tk), lambda b, h, q: (b, 0, h, 0)),
            pl.BlockSpec((1, ratio + 1, tk, tq), lambda b, h, q: (h, 0, 0, 0)),
            pl.BlockSpec((ATT_V_DIM, 1), lambda b, h, q: (0, 0)),
        ],
        out_specs=pl.BlockSpec((1, tq, ATT_V_DIM), lambda b, h, q: (b, q, h)),
        out_shape=jax.ShapeDtypeStruct((bsz, seq, ATT_WIDTH), BF16),
        scratch_shapes=[
            pltpu.VMEM((2 * ATT_QK_DIM, 2 * tq), BF16),
            pltpu.VMEM((1, 2 * tq), F32),
            pltpu.VMEM((1, 2 * tq), F32),
            pltpu.VMEM((ATT_V_DIM, 2 * tq), F32),
        ],
        compiler_params=pltpu.CompilerParams(
            dimension_semantics=("arbitrary", "arbitrary", "arbitrary"),
            vmem_limit_bytes=V7X_VMEM_LIMIT_BYTES),
        name="diffattn",
    )(lam, qT, k, vT, bias, g)


def _ssm_kernel(u_ref, wb_ref, wc_ref, lam_ref, d_ref, wglu_ref, bglu_ref, g_ref,
                o_ref, xs, st, *, steps):
    rows = V7X_SUBLANES
    half = xs.shape[2] // 2
    lanes = V7X_LANES

    @pl.when(pl.program_id(0) == 0)
    def _():
        st[...] = jnp.zeros_like(st)

    u = u_ref[...]
    ub = u.astype(BF16)
    ys = []
    for c in range(SSM_LANE_GROUPS):
        xs[c] = _dot(ub[:, c * lanes:(c + 1) * lanes], wb_ref[c])
        lre = lam_ref[c, :, 0:half]
        lim = lam_ref[c, :, half:2 * half]

        def step(t, carry, c=c, lre=lre, lim=lim):
            sre, sim = carry
            r = pl.multiple_of(t * rows, rows)
            bre = xs[c, pl.ds(r, rows), 0:half]
            bim = xs[c, pl.ds(r, rows), half:2 * half]
            nre = lre * sre - lim * sim + bre
            nim = lre * sim + lim * sre + bim
            xs[c, pl.ds(r, rows), 0:half] = nre
            xs[c, pl.ds(r, rows), half:2 * half] = nim
            return nre, nim

        sre, sim = lax.fori_loop(0, steps, step,
                                 (st[c, :, 0:half], st[c, :, half:2 * half]), unroll=8)
        st[c, :, 0:half] = sre
        st[c, :, half:2 * half] = sim
        ys.append(_dot(xs[c].astype(BF16), wc_ref[c]))

    y = jnp.concatenate(ys, axis=1) + d_ref[...] * u
    gl = jax.nn.gelu(y)
    z = _dot(gl.astype(BF16), wglu_ref[...]) + bglu_ref[...]
    o_ref[...] = _rms(gl * jax.nn.sigmoid(z), g_ref[...]).astype(BF16)


def _ssm(u2, wb, wc, lam, dskip, wglu, bglu, g, *, steps):
    rows_total, width = u2.shape
    r = steps * V7X_SUBLANES
    const = lambda shape: pl.BlockSpec(shape, lambda i: (0,) * len(shape),
                                       pipeline_mode=pl.Buffered(1))
    return pl.pallas_call(
        functools.partial(_ssm_kernel, steps=steps),
        grid=(rows_total // r,),
        in_specs=[
            pl.BlockSpec((r, width), lambda i: (i, 0)),
            const(wb.shape), const(wc.shape), const(lam.shape), const(dskip.shape),
            const(wglu.shape), const(bglu.shape), const(g.shape),
        ],
        out_specs=pl.BlockSpec((r, width), lambda i: (i, 0)),
        out_shape=jax.ShapeDtypeStruct((rows_total, width), BF16),
        scratch_shapes=[
            pltpu.VMEM((SSM_LANE_GROUPS, r, wb.shape[2]), F32),
            pltpu.VMEM((SSM_LANE_GROUPS, V7X_SUBLANES, wb.shape[2]), F32),
        ],
        compiler_params=pltpu.CompilerParams(
            dimension_semantics=("arbitrary",),
            vmem_limit_bytes=V7X_VMEM_LIMIT_BYTES),
        name="s5ssm",
    )(u2, wb, wc, lam, dskip, wglu, bglu, g)


def _ffn_kernel(x_ref, att_ref, ssm_ref, woa_ref, wos_ref, gf_ref, wg_ref, wu_ref, wd_ref,
                gl_ref, o_ref):
    x1 = x_ref[0] + _dot(att_ref[0], woa_ref[...]) + _dot(ssm_ref[...], wos_ref[...])
    h = _rms(x1, gf_ref[...]).astype(BF16)
    a = (jax.nn.silu(_dot(h, wg_ref[...])) * _dot(h, wu_ref[...])).astype(BF16)
    o_ref[0] = _rms(x1 + _dot(a, wd_ref[...]), gl_ref[...])


def _ffn(x, att, ssm2, woa, wos, gf, wg, wu, wd, gl, *, tm):
    bsz, seq, d = x.shape
    const = lambda shape: pl.BlockSpec(shape, lambda b, s: (0,) * len(shape),
                                       pipeline_mode=pl.Buffered(1))
    return pl.pallas_call(
        _ffn_kernel,
        grid=(bsz, seq // tm),
        in_specs=[
            pl.BlockSpec((1, tm, d), lambda b, s: (b, s, 0)),
            pl.BlockSpec((1, tm, ATT_WIDTH), lambda b, s: (b, s, 0)),
            pl.BlockSpec((tm, SSM_WIDTH), lambda b, s: (s, b)),
            const(woa.shape), const(wos.shape), const(gf.shape),
            const(wg.shape), const(wu.shape), const(wd.shape), const(gl.shape),
        ],
        out_specs=pl.BlockSpec((1, tm, d), lambda b, s: (b, s, 0)),
        out_shape=jax.ShapeDtypeStruct((bsz, seq, d), x.dtype),
        compiler_params=pltpu.CompilerParams(
            dimension_semantics=("arbitrary", "arbitrary"),
            vmem_limit_bytes=V7X_VMEM_LIMIT_BYTES),
        name="outproj_ffn",
    )(x, att, ssm2, woa, wos, gf, wg, wu, wd, gl)


def _t5_bucket(n):
    max_exact = REL_BUCKETS // 2
    nf = jnp.maximum(n, 1).astype(F32)
    large = max_exact + (jnp.log(nf / max_exact) / math.log(REL_MAX_DIST / max_exact)
                         * (REL_BUCKETS - max_exact)).astype(jnp.int32)
    large = jnp.minimum(large, REL_BUCKETS - 1)
    return jnp.where(n < max_exact, n, large)


def _bias_tiles(rel_bias, tq, tk):
    assert tk >= REL_MAX_DIST
    nd = tq + tk
    table = rel_bias.astype(F32)
    by_dist = table[_t5_bucket(jnp.arange(nd))]
    far = table[_t5_bucket(jnp.full((), REL_MAX_DIST, jnp.int32))]
    by_dist = (by_dist - far[None, :]) * LOG2E
    r = np.arange(tq // tk + 1)[:, None, None]
    jj = np.arange(tk)[None, :, None]
    i = np.arange(tq)[None, None, :]
    dist = i - (r - 1) * tk - jj
    tiles = jnp.where((dist >= 0)[..., None], by_dist[np.clip(dist, 0, nd - 1)], MASK_VALUE)
    return jnp.transpose(tiles, (3, 0, 1, 2))


def _ssm_params(A_re, A_im, log_dt, B_re, B_im, C_re, C_im):
    lam = lax.complex(A_re.astype(F32), A_im.astype(F32))
    dt = jnp.exp(log_dt.astype(F32))[:, None]
    lam_bar = jnp.exp(lam * dt)
    b_bar = ((lam_bar - 1.0) / lam)[:, :, None] * lax.complex(B_re.astype(F32), B_im.astype(F32))
    nlg = SSM_LANE_GROUPS
    gpl = SSM_GROUPS // nlg
    eye = jnp.eye(gpl, dtype=F32)

    def in_blocks(w):
        w = w.reshape(nlg, gpl, SSM_STATE, SSM_GROUP)
        return jnp.einsum('kgnc,gh->kgchn', w, eye).reshape(nlg, gpl * SSM_GROUP, gpl * SSM_STATE)

    def out_blocks(w):
        w = w.reshape(nlg, gpl, SSM_GROUP, SSM_STATE)
        return jnp.einsum('kgcn,gh->kgnhc', w, eye).reshape(nlg, gpl * SSM_STATE, gpl * SSM_GROUP)

    wb = jnp.concatenate([in_blocks(b_bar.real), in_blocks(b_bar.imag)], axis=2).astype(BF16)
    wc = jnp.concatenate([out_blocks(C_re.astype(F32)), out_blocks(-C_im.astype(F32))],
                         axis=1).astype(BF16)
    lam_ri = jnp.concatenate([lam_bar.real.reshape(nlg, 1, -1), lam_bar.imag.reshape(nlg, 1, -1)],
                             axis=2)
    lam_ri = jnp.broadcast_to(lam_ri, (nlg, V7X_SUBLANES, lam_ri.shape[2]))
    return wb, wc, lam_ri


def kernel(x, norm_mix_g, w_in, lambda_q1, lambda_k1, lambda_q2, lambda_k2, subln_g, rel_bias,
           A_re, A_im, log_dt, B_re, B_im, C_re, C_im, D_skip, w_glu, b_glu, ssm_norm_g, w_out,
           norm_ffn_g, w_gate, w_up, w_down, norm_final_g):
    bsz, seq, d = x.shape
    assert d == D_MODEL and bsz == V7X_SUBLANES
    ts, tq, tk, tm = PROJ_ROWS, ATT_TQ, ATT_TK, FFN_ROWS
    assert seq % ts == 0 and seq % tq == 0 and seq % tm == 0 and seq % SSM_STEPS == 0
    assert ts % tk == 0 and tq % tk == 0
    l = 0
    row = lambda v: v.astype(F32).reshape(1, -1)

    w = w_in[l].astype(F32)
    qk_scale = ATT_QK_DIM ** -0.5 * LOG2E
    wqT = (w[:, 0:QK_COLS] * qk_scale).T.astype(BF16)
    wk = w[:, QK_COLS:2 * QK_COLS].astype(BF16)
    wvT = w[:, 2 * QK_COLS:2 * QK_COLS + ATT_WIDTH].T.astype(BF16)
    wu = w[:, 2 * QK_COLS + ATT_WIDTH:].astype(BF16)
    lam_init = 0.8 - 0.6 * math.exp(-0.3 * l)
    lam = (jnp.exp(jnp.sum(lambda_q1[l].astype(F32) * lambda_k1[l].astype(F32)))
           - jnp.exp(jnp.sum(lambda_q2[l].astype(F32) * lambda_k2[l].astype(F32)))
           + lam_init).reshape(1, 1)
    bias = _bias_tiles(rel_bias, tq, tk)
    wb, wc, lam_ri = _ssm_params(A_re[l], A_im[l], log_dt[l], B_re[l], B_im[l], C_re[l], C_im[l])

    qT, k, vT, u = _inproj(x, row(norm_mix_g[l]), wqT, wk, wvT, wu, ts=ts, tk=tk)
    att = _attention(lam, qT, k, vT, bias, subln_g[l].astype(F32).reshape(-1, 1),
                     tq=tq, tk=tk, out_scale=1.0 - lam_init)
    ssm = _ssm(u.reshape(seq * bsz, SSM_WIDTH), wb, wc, lam_ri, row(D_skip[l]),
               w_glu[l].astype(BF16), row(b_glu[l]), row(ssm_norm_g[l]), steps=SSM_STEPS)

    wo = w_out[l].astype(BF16)
    return _ffn(x, att, ssm.reshape(seq, bsz * SSM_WIDTH), wo[0:ATT_WIDTH], wo[ATT_WIDTH:],
                row(norm_ffn_g[l]), w_gate[l].astype(BF16), w_up[l].astype(BF16),
                w_down[l].astype(BF16), row(norm_final_g), tm=tm)
```

```python
import functools
import math

import jax
import jax.numpy as jnp
import numpy as np
from jax import lax
from jax.experimental import pallas as pl
from jax.experimental.pallas import tpu as pltpu

F32 = jnp.float32
BF16 = jnp.bfloat16

D_MODEL = 1024
ATT_WIDTH = 512
SSM_WIDTH = 512
ATT_HEADS = 4
ATT_QK_DIM = 64
ATT_V_DIM = 128
QK_COLS = 512
SSM_GROUP = 16
SSM_GROUPS = 32
SSM_STATE = 64
REL_BUCKETS = 32
REL_MAX_DIST = 128
EPS = 1e-6
LOG2E = math.log2(math.e)
MASK_VALUE = -1e30

V7X_SUBLANES = 8
V7X_LANES = 128
V7X_VMEM_LIMIT_BYTES = 56 * 1024 * 1024

PROJ_ROWS = 512
ATT_TQ = 512
ATT_TK = 256
ATT_COLS = 256
SSM_STEPS = 64
SSM_LANE_GROUPS = 4
FFN_ROWS = 512

_NT = (((1,), (1,)), ((), ()))


def _dot(a, b):
    return jnp.dot(a, b, preferred_element_type=F32)


def _rms(x, g):
    ms = jnp.mean(x * x, axis=-1, keepdims=True)
    return x * lax.rsqrt(ms + EPS) * g


def _inproj_kernel(x_ref, g_ref, wqT_ref, wk_ref, wvT_ref, wu_ref,
                   qT_ref, k_ref, vT_ref, u_ref, *, tk):
    h = _rms(x_ref[0], g_ref[...]).astype(BF16)
    qT = lax.dot_general(wqT_ref[...], h, _NT, preferred_element_type=F32)
    qT_ref[0] = qT.astype(BF16)
    k_ref[0] = _dot(h, wk_ref[...]).astype(BF16)
    vT = lax.dot_general(wvT_ref[...], h, _NT, preferred_element_type=F32).astype(BF16)
    for c in range(vT_ref.shape[1]):
        vT_ref[0, c] = vT[:, c * tk:(c + 1) * tk]
    u_ref[...] = _dot(h, wu_ref[...])


def _inproj(x, g, wqT, wk, wvT, wu, *, ts, tk):
    bsz, seq, d = x.shape
    ns = seq // ts
    const = lambda shape: pl.BlockSpec(shape, lambda b, s: (0,) * len(shape),
                                       pipeline_mode=pl.Buffered(1))
    return pl.pallas_call(
        functools.partial(_inproj_kernel, tk=tk),
        grid=(bsz, ns),
        in_specs=[
            pl.BlockSpec((1, ts, d), lambda b, s: (b, s, 0)),
            const((1, d)),
            const((QK_COLS, d)),
            const((d, QK_COLS)),
            const((ATT_WIDTH, d)),
            const((d, SSM_WIDTH)),
        ],
        out_specs=[
            pl.BlockSpec((1, QK_COLS, ts), lambda b, s: (b, 0, s)),
            pl.BlockSpec((1, ts, QK_COLS), lambda b, s: (b, s, 0)),
            pl.BlockSpec((1, ts // tk, ATT_WIDTH, tk), lambda b, s: (b, s, 0, 0)),
            pl.BlockSpec((ts, SSM_WIDTH), lambda b, s: (s, b)),
        ],
        out_shape=[
            jax.ShapeDtypeStruct((bsz, QK_COLS, seq), BF16),
            jax.ShapeDtypeStruct((bsz, seq, QK_COLS), BF16),
            jax.ShapeDtypeStruct((bsz, seq // tk, ATT_WIDTH, tk), BF16),
            jax.ShapeDtypeStruct((seq, bsz * SSM_WIDTH), F32),
        ],
        compiler_params=pltpu.CompilerParams(
            dimension_semantics=("arbitrary", "arbitrary"),
            vmem_limit_bytes=V7X_VMEM_LIMIT_BYTES),
        name="inproj",
    )(x, g, wqT, wk, wvT, wu)


def _attn_kernel(lam_ref, qT_ref, k_ref, vT_ref, bias_ref, g_ref, o_ref,
                 qbd, s_a, s_b, m_s, l_s, acc_s, *, tq, tk, out_scale):
    qi = pl.program_id(2)
    ratio = tq // tk
    dk = ATT_QK_DIM
    bufs = (s_a, s_b)

    qbd[...] = jnp.zeros_like(qbd)
    qbd[0:dk, 0:tq] = qT_ref[0, 0:dk, :]
    qbd[dk:2 * dk, tq:2 * tq] = qT_ref[0, dk:2 * dk, :]
    m_s[...] = jnp.full_like(m_s, -jnp.inf)
    l_s[...] = jnp.zeros_like(l_s)
    acc_s[...] = jnp.zeros_like(acc_s)

    def logits(j, buf):
        r0 = pl.multiple_of(j * tk, tk)
        buf[...] = _dot(k_ref[0, pl.ds(r0, tk), :], qbd[...])

    def softmax_pv(j, buf, bias):
        vt = vT_ref[0, j]
        for c0 in range(0, 2 * tq, ATT_COLS):
            cols = slice(c0, c0 + ATT_COLS)
            s = buf[:, cols]
            if bias is not None:
                s = s + bias[:, c0 % tq:c0 % tq + ATT_COLS]
            m_old = m_s[:, cols]
            m_new = jnp.maximum(m_old, jnp.max(s, axis=0, keepdims=True))
            alpha = jnp.exp2(m_old - m_new)
            p = jnp.exp2(s - m_new)
            l_s[:, cols] = alpha * l_s[:, cols] + jnp.sum(p, axis=0, keepdims=True)
            acc_s[:, cols] = alpha * acc_s[:, cols] + _dot(vt, p.astype(BF16))
            m_s[:, cols] = m_new

    n_far = jnp.maximum(qi * ratio - 1, 0)

    @pl.when(n_far > 0)
    def _():
        logits(0, s_a)

        def pair(p, carry):
            logits(2 * p + 1, s_b)
            softmax_pv(2 * p, s_a, None)
            logits(2 * p + 2, s_a)
            softmax_pv(2 * p + 1, s_b, None)
            return carry

        npairs = (n_far - 1) // 2
        lax.fori_loop(0, npairs, pair, 0)
        t = 2 * npairs

        @pl.when(n_far - t == 2)
        def _():
            logits(t + 1, s_b)
            softmax_pv(t, s_a, None)
            softmax_pv(t + 1, s_b, None)

        @pl.when(n_far - t == 1)
        def _():
            softmax_pv(t, s_a, None)

    @pl.when(qi > 0)
    def _():
        logits(qi * ratio - 1, s_a)
        softmax_pv(qi * ratio - 1, s_a, bias_ref[0, 0])

    for r in range(1, ratio + 1):
        j = qi * ratio + (r - 1)
        logits(j, bufs[r % 2])
        softmax_pv(j, bufs[r % 2], bias_ref[0, r])

    lam = lam_ref[0, 0]
    acc = acc_s[...]
    l = l_s[...]
    o = acc[:, 0:tq] / l[:, 0:tq] - lam * (acc[:, tq:2 * tq] / l[:, tq:2 * tq])
    ms = jnp.mean(o * o, axis=0, keepdims=True)
    o = o * lax.rsqrt(ms + EPS) * g_ref[...] * out_scale
    o_ref[0] = o.T.astype(BF16)


def _attention(lam, qT, k, vT, bias, g, *, tq, tk, out_scale):
    bsz, seq, _ = k.shape
    nq = seq // tq
    ratio = tq // tk
    return pl.pallas_call(
        functools.partial(_attn_kernel, tq=tq, tk=tk, out_scale=out_scale),
        grid=(bsz, ATT_HEADS, nq),
        in_specs=[
            pl.BlockSpec(memory_space=pltpu.SMEM),
            pl.BlockSpec((1, 2 * ATT_QK_DIM, tq), lambda b, h, q: (b, h, q)),
            pl.BlockSpec((1, seq, 2 * ATT_QK_DIM), lambda b, h, q: (b, 0, h)),
            pl.BlockSpec((1, seq // tk, ATT_V_DIM, tk), lambda b, h, q: (b, 0, h, 0)),
            pl.BlockSpec((1, ratio + 1, tk, tq), lambda b, h, q: (h, 0, 0, 0)),
            pl.BlockSpec((ATT_V_DIM, 1), lambda b, h, q: (0, 0)),
        ],
        out_specs=pl.BlockSpec((1, tq, ATT_V_DIM), lambda b, h, q: (b, q, h)),
        out_shape=jax.ShapeDtypeStruct((bsz, seq, ATT_WIDTH), BF16),
        scratch_shapes=[
            pltpu.VMEM((2 * ATT_QK_DIM, 2 * tq), BF16),
            pltpu.VMEM((tk, 2 * tq), F32),
            pltpu.VMEM((tk, 2 * tq), F32),
            pltpu.VMEM((1, 2 * tq), F32),
            pltpu.VMEM((1, 2 * tq), F32),
            pltpu.VMEM((ATT_V_DIM, 2 * tq), F32),
        ],
        compiler_params=pltpu.CompilerParams(
            dimension_semantics=("arbitrary", "arbitrary", "arbitrary"),
            vmem_limit_bytes=V7X_VMEM_LIMIT_BYTES),
        name="diffattn",
    )(lam, qT, k, vT, bias, g)


def _ssm_kernel(u_ref, wb_ref, wc_ref, lam_ref, d_ref, wglu_ref, bglu_ref, g_ref,
                o_ref, xs, st, *, steps):
    rows = V7X_SUBLANES
    half = xs.shape[2] // 2
    lanes = V7X_LANES

    @pl.when(pl.program_id(0) == 0)
    def _():
        st[...] = jnp.zeros_like(st)

    u = u_ref[...]
    ub = u.astype(BF16)
    ys = []
    for c in range(SSM_LANE_GROUPS):
        xs[c] = _dot(ub[:, c * lanes:(c + 1) * lanes], wb_ref[c])
        lre = lam_ref[c, :, 0:half]
        lim = lam_ref[c, :, half:2 * half]

        def step(t, carry, c=c, lre=lre, lim=lim):
            sre, sim = carry
            r = pl.multiple_of(t * rows, rows)
            bre = xs[c, pl.ds(r, rows), 0:half]
            bim = xs[c, pl.ds(r, rows), half:2 * half]
            nre = lre * sre - lim * sim + bre
            nim = lre * sim + lim * sre + bim
            xs[c, pl.ds(r, rows), 0:half] = nre
            xs[c, pl.ds(r, rows), half:2 * half] = nim
            return nre, nim

        sre, sim = lax.fori_loop(0, steps, step,
                                 (st[c, :, 0:half], st[c, :, half:2 * half]), unroll=8)
        st[c, :, 0:half] = sre
        st[c, :, half:2 * half] = sim
        ys.append(_dot(xs[c].astype(BF16), wc_ref[c]))

    y = jnp.concatenate(ys, axis=1) + d_ref[...] * u
    gl = jax.nn.gelu(y)
    z = _dot(gl.astype(BF16), wglu_ref[...]) + bglu_ref[...]
    o_ref[...] = _rms(gl * jax.nn.sigmoid(z), g_ref[...]).astype(BF16)


def _ssm(u2, wb, wc, lam, dskip, wglu, bglu, g, *, steps):
    rows_total, width = u2.shape
    r = steps * V7X_SUBLANES
    const = lambda shape: pl.BlockSpec(shape, lambda i: (0,) * len(shape),
                                       pipeline_mode=pl.Buffered(1))
    return pl.pallas_call(
        functools.partial(_ssm_kernel, steps=steps),
        grid=(rows_total // r,),
        in_specs=[
            pl.BlockSpec((r, width), lambda i: (i, 0)),
            const(wb.shape), const(wc.shape), const(lam.shape), const(dskip.shape),
            const(wglu.shape), const(bglu.shape), const(g.shape),
        ],
        out_specs=pl.BlockSpec((r, width), lambda i: (i, 0)),
        out_shape=jax.ShapeDtypeStruct((rows_total, width), BF16),
        scratch_shapes=[
            pltpu.VMEM((SSM_LANE_GROUPS, r, wb.shape[2]), F32),
            pltpu.VMEM((SSM_LANE_GROUPS, V7X_SUBLANES, wb.shape[2]), F32),
        ],
        compiler_params=pltpu.CompilerParams(
            dimension_semantics=("arbitrary",),
            vmem_limit_bytes=V7X_VMEM_LIMIT_BYTES),
        name="s5ssm",
    )(u2, wb, wc, lam, dskip, wglu, bglu, g)


def _ffn_kernel(x_ref, att_ref, ssm_ref, woa_ref, wos_ref, gf_ref, wg_ref, wu_ref, wd_ref,
                gl_ref, o_ref):
    x1 = x_ref[0] + _dot(att_ref[0], woa_ref[...]) + _dot(ssm_ref[...], wos_ref[...])
    h = _rms(x1, gf_ref[...]).astype(BF16)
    a = (jax.nn.silu(_dot(h, wg_ref[...])) * _dot(h, wu_ref[...])).astype(BF16)
    o_ref[0] = _rms(x1 + _dot(a, wd_ref[...]), gl_ref[...])


def _ffn(x, att, ssm2, woa, wos, gf, wg, wu, wd, gl, *, tm):
    bsz, seq, d = x.shape
    const = lambda shape: pl.BlockSpec(shape, lambda b, s: (0,) * len(shape),
                                       pipeline_mode=pl.Buffered(1))
    return pl.pallas_call(
        _ffn_kernel,
        grid=(bsz, seq // tm),
        in_specs=[
            pl.BlockSpec((1, tm, d), lambda b, s: (b, s, 0)),
            pl.BlockSpec((1, tm, ATT_WIDTH), lambda b, s: (b, s, 0)),
            pl.BlockSpec((tm, SSM_WIDTH), lambda b, s: (s, b)),
            const(woa.shape), const(wos.shape), const(gf.shape),
            const(wg.shape), const(wu.shape), const(wd.shape), const(gl.shape),
        ],
        out_specs=pl.BlockSpec((1, tm, d), lambda b, s: (b, s, 0)),
        out_shape=jax.ShapeDtypeStruct((bsz, seq, d), x.dtype),
        compiler_params=pltpu.CompilerParams(
            dimension_semantics=("arbitrary", "arbitrary"),
            vmem_limit_bytes=V7X_VMEM_LIMIT_BYTES),
        name="outproj_ffn",
    )(x, att, ssm2, woa, wos, gf, wg, wu, wd, gl)


def _t5_bucket(n):
    max_exact = REL_BUCKETS // 2
    nf = jnp.maximum(n, 1).astype(F32)
    large = max_exact + (jnp.log(nf / max_exact) / math.log(REL_MAX_DIST / max_exact)
                         * (REL_BUCKETS - max_exact)).astype(jnp.int32)
    large = jnp.minimum(large, REL_BUCKETS - 1)
    return jnp.where(n < max_exact, n, large)


def _bias_tiles(rel_bias, tq, tk):
    assert tk >= REL_MAX_DIST
    ratio = tq // tk
    nd = tq + tk + 1
    neg = ratio * tk
    table = rel_bias.astype(F32)
    by_dist = table[_t5_bucket(jnp.arange(nd))]
    far = table[_t5_bucket(jnp.full((), REL_MAX_DIST, jnp.int32))]
    by_dist = (by_dist - far[None, :]) * LOG2E
    by_dist = jnp.concatenate([jnp.full((neg, ATT_HEADS), MASK_VALUE, F32), by_dist], axis=0).T
    m = tk + tq
    tiles = []
    for r in range(ratio + 1):
        start = neg - (tk - 1) - (r - 1) * tk
        c = by_dist[:, start:start + m]
        t = jnp.broadcast_to(c[:, None, :], (ATT_HEADS, tk, m)).reshape(ATT_HEADS, tk * m)
        t = t[:, :tk * (m - 1)].reshape(ATT_HEADS, tk, m - 1)
        tiles.append(t[:, :, tk - 1:tk - 1 + tq])
    return jnp.stack(tiles, axis=1)


def _ssm_params(A_re, A_im, log_dt, B_re, B_im, C_re, C_im):
    lam = lax.complex(A_re.astype(F32), A_im.astype(F32))
    dt = jnp.exp(log_dt.astype(F32))[:, None]
    lam_bar = jnp.exp(lam * dt)
    b_bar = ((lam_bar - 1.0) / lam)[:, :, None] * lax.complex(B_re.astype(F32), B_im.astype(F32))
    nlg = SSM_LANE_GROUPS
    gpl = SSM_GROUPS // nlg
    eye = jnp.eye(gpl, dtype=F32)

    def in_blocks(w):
        w = w.reshape(nlg, gpl, SSM_STATE, SSM_GROUP)
        return jnp.einsum('kgnc,gh->kgchn', w, eye).reshape(nlg, gpl * SSM_GROUP, gpl * SSM_STATE)

    def out_blocks(w):
        w = w.reshape(nlg, gpl, SSM_GROUP, SSM_STATE)
        return jnp.einsum('kgcn,gh->kgnhc', w, eye).reshape(nlg, gpl * SSM_STATE, gpl * SSM_GROUP)

    wb = jnp.concatenate([in_blocks(b_bar.real), in_blocks(b_bar.imag)], axis=2).astype(BF16)
    wc = jnp.concatenate([out_blocks(C_re.astype(F32)), out_blocks(-C_im.astype(F32))],
                         axis=1).astype(BF16)
    lam_ri = jnp.concatenate([lam_bar.real.reshape(nlg, 1, -1), lam_bar.imag.reshape(nlg, 1, -1)],
                             axis=2)
    lam_ri = jnp.broadcast_to(lam_ri, (nlg, V7X_SUBLANES, lam_ri.shape[2]))
    return wb, wc, lam_ri


def kernel(x, norm_mix_g, w_in, lambda_q1, lambda_k1, lambda_q2, lambda_k2, subln_g, rel_bias,
           A_re, A_im, log_dt, B_re, B_im, C_re, C_im, D_skip, w_glu, b_glu, ssm_norm_g, w_out,
           norm_ffn_g, w_gate, w_up, w_down, norm_final_g):
    bsz, seq, d = x.shape
    assert d == D_MODEL and bsz == V7X_SUBLANES
    ts, tq, tk, tm = PROJ_ROWS, ATT_TQ, ATT_TK, FFN_ROWS
    assert seq % ts == 0 and seq % tq == 0 and seq % tm == 0 and seq % SSM_STEPS == 0
    assert ts % tk == 0 and tq % tk == 0
    l = 0
    row = lambda v: v.astype(F32).reshape(1, -1)

    w = w_in[l].astype(F32)
    qk_scale = ATT_QK_DIM ** -0.5 * LOG2E
    wqT = (w[:, 0:QK_COLS] * qk_scale).T.astype(BF16)
    wk = w[:, QK_COLS:2 * QK_COLS].astype(BF16)
    wvT = w[:, 2 * QK_COLS:2 * QK_COLS + ATT_WIDTH].T.astype(BF16)
    wu = w[:, 2 * QK_COLS + ATT_WIDTH:].astype(BF16)
    lam_init = 0.8 - 0.6 * math.exp(-0.3 * l)
    lam = (jnp.exp(jnp.sum(lambda_q1[l].astype(F32) * lambda_k1[l].astype(F32)))
           - jnp.exp(jnp.sum(lambda_q2[l].astype(F32) * lambda_k2[l].astype(F32)))
           + lam_init).reshape(1, 1)
    bias = _bias_tiles(rel_bias, tq, tk)
    wb, wc, lam_ri = _ssm_params(A_re[l], A_im[l], log_dt[l], B_re[l], B_im[l], C_re[l], C_im[l])

    qT, k, vT, u = _inproj(x, row(norm_mix_g[l]), wqT, wk, wvT, wu, ts=ts, tk=tk)
    att = _attention(lam, qT, k, vT, bias, subln_g[l].astype(F32).reshape(-1, 1),
                     tq=tq, tk=tk, out_scale=1.0 - lam_init)
    ssm = _ssm(u.reshape(seq * bsz, SSM_WIDTH), wb, wc, lam_ri, row(D_skip[l]),
               w_glu[l].astype(BF16), row(b_glu[l]), row(ssm_norm_g[l]), steps=SSM_STEPS)

    wo = w_out[l].astype(BF16)
    return _ffn(x, att, ssm.reshape(seq, bsz * SSM_WIDTH), wo[0:ATT_WIDTH], wo[ATT_WIDTH:],
                row(norm_ffn_g[l]), w_gate[l].astype(BF16), w_up[l].astype(BF16),
                w_down[l].astype(BF16), row(norm_final_g), tm=tm)
```

```python
import functools
import math

import jax
import jax.numpy as jnp
import numpy as np
from jax import lax
from jax.experimental import pallas as pl
from jax.experimental.pallas import tpu as pltpu

F32 = jnp.float32
BF16 = jnp.bfloat16

D_MODEL = 1024
ATT_WIDTH = 512
SSM_WIDTH = 512
ATT_HEADS = 4
ATT_QK_DIM = 64
ATT_V_DIM = 128
QK_COLS = 512
SSM_GROUP = 16
SSM_GROUPS = 32
SSM_STATE = 64
REL_BUCKETS = 32
REL_MAX_DIST = 128
EPS = 1e-6
LOG2E = math.log2(math.e)
MASK_VALUE = -1e30

V7X_SUBLANES = 8
V7X_LANES = 128
V7X_VMEM_LIMIT_BYTES = 56 * 1024 * 1024

PROJ_ROWS = 512
ATT_TQ = 512
ATT_TK = 512
ATT_COLS = 256
SSM_STEPS = 64
SSM_LANE_GROUPS = 4
FFN_ROWS = 512

_NT = (((1,), (1,)), ((), ()))


def _dot(a, b):
    return jnp.dot(a, b, preferred_element_type=F32)


def _rms(x, g):
    ms = jnp.mean(x * x, axis=-1, keepdims=True)
    return x * lax.rsqrt(ms + EPS) * g


def _inproj_kernel(x_ref, g_ref, wqT_ref, wk_ref, wvT_ref, wu_ref,
                   qT_ref, k_ref, vT_ref, u_ref, *, tk):
    h = _rms(x_ref[0], g_ref[...]).astype(BF16)
    qT = lax.dot_general(wqT_ref[...], h, _NT, preferred_element_type=F32)
    qT_ref[0] = qT.astype(BF16)
    k_ref[0] = _dot(h, wk_ref[...]).astype(BF16)
    vT = lax.dot_general(wvT_ref[...], h, _NT, preferred_element_type=F32).astype(BF16)
    for c in range(vT_ref.shape[1]):
        vT_ref[0, c] = vT[:, c * tk:(c + 1) * tk]
    u_ref[...] = _dot(h, wu_ref[...])


def _inproj(x, g, wqT, wk, wvT, wu, *, ts, tk):
    bsz, seq, d = x.shape
    ns = seq // ts
    const = lambda shape: pl.BlockSpec(shape, lambda b, s: (0,) * len(shape),
                                       pipeline_mode=pl.Buffered(1))
    return pl.pallas_call(
        functools.partial(_inproj_kernel, tk=tk),
        grid=(bsz, ns),
        in_specs=[
            pl.BlockSpec((1, ts, d), lambda b, s: (b, s, 0)),
            const((1, d)),
            const((QK_COLS, d)),
            const((d, QK_COLS)),
            const((ATT_WIDTH, d)),
            const((d, SSM_WIDTH)),
        ],
        out_specs=[
            pl.BlockSpec((1, QK_COLS, ts), lambda b, s: (b, 0, s)),
            pl.BlockSpec((1, ts, QK_COLS), lambda b, s: (b, s, 0)),
            pl.BlockSpec((1, ts // tk, ATT_WIDTH, tk), lambda b, s: (b, s, 0, 0)),
            pl.BlockSpec((ts, SSM_WIDTH), lambda b, s: (s, b)),
        ],
        out_shape=[
            jax.ShapeDtypeStruct((bsz, QK_COLS, seq), BF16),
            jax.ShapeDtypeStruct((bsz, seq, QK_COLS), BF16),
            jax.ShapeDtypeStruct((bsz, seq // tk, ATT_WIDTH, tk), BF16),
            jax.ShapeDtypeStruct((seq, bsz * SSM_WIDTH), F32),
        ],
        compiler_params=pltpu.CompilerParams(
            dimension_semantics=("arbitrary", "arbitrary"),
            vmem_limit_bytes=V7X_VMEM_LIMIT_BYTES),
        name="inproj",
    )(x, g, wqT, wk, wvT, wu)


def _attn_kernel(lam_ref, qT_ref, k_ref, vT_ref, bias_ref, g_ref, o_ref,
                 qbd, s_a, s_b, m_s, l_s, acc_s, *, tq, tk, out_scale):
    qi = pl.program_id(2)
    ratio = tq // tk
    dk = ATT_QK_DIM
    bufs = (s_a, s_b)

    qbd[...] = jnp.zeros_like(qbd)
    qbd[0:dk, 0:tq] = qT_ref[0, 0:dk, :]
    qbd[dk:2 * dk, tq:2 * tq] = qT_ref[0, dk:2 * dk, :]
    m_s[...] = jnp.full_like(m_s, -jnp.inf)
    l_s[...] = jnp.zeros_like(l_s)
    acc_s[...] = jnp.zeros_like(acc_s)

    def live_rows(r, c0):
        if r is None or r == 0:
            return tk
        return max(0, min(tk, c0 % tq + ATT_COLS - (r - 1) * tk))

    def logits(tile, buf):
        j, r = tile
        r0 = pl.multiple_of(j * tk, tk)
        if r is None or r == 0:
            buf[...] = _dot(k_ref[0, pl.ds(r0, tk), :], qbd[...])
            return
        for c0 in range(0, 2 * tq, ATT_COLS):
            rows = live_rows(r, c0)
            if rows:
                buf[0:rows, c0:c0 + ATT_COLS] = _dot(k_ref[0, pl.ds(r0, rows), :],
                                                     qbd[:, c0:c0 + ATT_COLS])

    def softmax_pv(tile, buf):
        j, r = tile
        for c0 in range(0, 2 * tq, ATT_COLS):
            rows = live_rows(r, c0)
            if rows == 0:
                continue
            cols = slice(c0, c0 + ATT_COLS)
            s = buf[0:rows, cols]
            if r is not None:
                s = s + bias_ref[0, r, 0:rows, c0 % tq:c0 % tq + ATT_COLS]
            m_old = m_s[:, cols]
            m_new = jnp.maximum(m_old, jnp.max(s, axis=0, keepdims=True))
            alpha = jnp.exp2(m_old - m_new)
            p = jnp.exp2(s - m_new)
            l_s[:, cols] = alpha * l_s[:, cols] + jnp.sum(p, axis=0, keepdims=True)
            acc_s[:, cols] = alpha * acc_s[:, cols] + _dot(vT_ref[0, j, :, 0:rows],
                                                           p.astype(BF16))
            m_s[:, cols] = m_new

    def run_tiles(tiles, preloaded):
        if not preloaded:
            logits(tiles[0], bufs[0])
        for i, tile in enumerate(tiles):
            if i + 1 < len(tiles):
                logits(tiles[i + 1], bufs[(i + 1) % 2])
            softmax_pv(tile, bufs[i % 2])

    diag = [(qi * ratio + (r - 1), r) for r in range(1, ratio + 1)]

    @pl.when(qi == 0)
    def _():
        run_tiles(diag, preloaded=False)

    @pl.when(qi > 0)
    def _():
        n_pre = qi * ratio
        logits((0, None), s_a)

        def pair(p, carry):
            logits((2 * p + 1, None), s_b)
            softmax_pv((2 * p, None), s_a)
            logits((2 * p + 2, None), s_a)
            softmax_pv((2 * p + 1, None), s_b)
            return carry

        npairs = (n_pre - 1) // 2
        lax.fori_loop(0, npairs, pair, 0)
        t = 2 * npairs

        @pl.when(n_pre - t == 1)
        def _():
            run_tiles([(t, 0)] + diag, preloaded=True)

        @pl.when(n_pre - t == 2)
        def _():
            run_tiles([(t, None), (t + 1, 0)] + diag, preloaded=True)

    lam = lam_ref[0, 0]
    acc = acc_s[...]
    l = l_s[...]
    o = acc[:, 0:tq] / l[:, 0:tq] - lam * (acc[:, tq:2 * tq] / l[:, tq:2 * tq])
    ms = jnp.mean(o * o, axis=0, keepdims=True)
    o = o * lax.rsqrt(ms + EPS) * g_ref[...] * out_scale
    o_ref[0] = o.T.astype(BF16)


def _attention(lam, qT, k, vT, bias, g, *, tq, tk, out_scale):
    bsz, seq, _ = k.shape
    nq = seq // tq
    ratio = tq // tk
    return pl.pallas_call(
        functools.partial(_attn_kernel, tq=tq, tk=tk, out_scale=out_scale),
        grid=(bsz, ATT_HEADS, nq),
        in_specs=[
            pl.BlockSpec(memory_space=pltpu.SMEM),
            pl.BlockSpec((1, 2 * ATT_QK_DIM, tq), lambda b, h, q: (b, h, q)),
            pl.BlockSpec((1, seq, 2 * ATT_QK_DIM), lambda b, h, q: (b, 0, h)),
            pl.BlockSpec((1, seq // tk, ATT_V_DIM, tk), lambda b, h, q: (b, 0, h, 0)),
            pl.BlockSpec((1, ratio + 1, tk, tq), lambda b, h, q: (h, 0, 0, 0)),
            pl.BlockSpec((ATT_V_DIM, 1), lambda b, h, q: (0, 0)),
        ],
        out_specs=pl.BlockSpec((1, tq, ATT_V_DIM), lambda b, h, q: (b, q, h)),
        out_shape=jax.ShapeDtypeStruct((bsz, seq, ATT_WIDTH), BF16),
        scratch_shapes=[
            pltpu.VMEM((2 * ATT_QK_DIM, 2 * tq), BF16),
            pltpu.VMEM((tk, 2 * tq), F32),
            pltpu.VMEM((tk, 2 * tq), F32),
            pltpu.VMEM((1, 2 * tq), F32),
            pltpu.VMEM((1, 2 * tq), F32),
            pltpu.VMEM((ATT_V_DIM, 2 * tq), F32),
        ],
        compiler_params=pltpu.CompilerParams(
            dimension_semantics=("arbitrary", "arbitrary", "arbitrary"),
            vmem_limit_bytes=V7X_VMEM_LIMIT_BYTES),
        name="diffattn",
    )(lam, qT, k, vT, bias, g)


def _ssm_kernel(u_ref, wb_ref, wc_ref, lam_ref, d_ref, wglu_ref, bglu_ref, g_ref,
                o_ref, xs, st, *, steps):
    rows = V7X_SUBLANES
    half = xs.shape[2] // 2
    lanes = V7X_LANES

    @pl.when(pl.program_id(0) == 0)
    def _():
        st[...] = jnp.zeros_like(st)

    u = u_ref[...]
    ub = u.astype(BF16)
    ys = []
    for c in range(SSM_LANE_GROUPS):
        xs[c] = _dot(ub[:, c * lanes:(c + 1) * lanes], wb_ref[c])
        lre = lam_ref[c, :, 0:half]
        lim = lam_ref[c, :, half:2 * half]

        def step(t, carry, c=c, lre=lre, lim=lim):
            sre, sim = carry
            r = pl.multiple_of(t * rows, rows)
            bre = xs[c, pl.ds(r, rows), 0:half]
            bim = xs[c, pl.ds(r, rows), half:2 * half]
            nre = lre * sre - lim * sim + bre
            nim = lre * sim + lim * sre + bim
            xs[c, pl.ds(r, rows), 0:half] = nre
            xs[c, pl.ds(r, rows), half:2 * half] = nim
            return nre, nim

        sre, sim = lax.fori_loop(0, steps, step,
                                 (st[c, :, 0:half], st[c, :, half:2 * half]), unroll=8)
        st[c, :, 0:half] = sre
        st[c, :, half:2 * half] = sim
        ys.append(_dot(xs[c].astype(BF16), wc_ref[c]))

    y = jnp.concatenate(ys, axis=1) + d_ref[...] * u
    gl = jax.nn.gelu(y)
    z = _dot(gl.astype(BF16), wglu_ref[...]) + bglu_ref[...]
    o_ref[...] = _rms(gl * jax.nn.sigmoid(z), g_ref[...]).astype(BF16)


def _ssm(u2, wb, wc, lam, dskip, wglu, bglu, g, *, steps):
    rows_total, width = u2.shape
    r = steps * V7X_SUBLANES
    const = lambda shape: pl.BlockSpec(shape, lambda i: (0,) * len(shape),
                                       pipeline_mode=pl.Buffered(1))
    return pl.pallas_call(
        functools.partial(_ssm_kernel, steps=steps),
        grid=(rows_total // r,),
        in_specs=[
            pl.BlockSpec((r, width), lambda i: (i, 0)),
            const(wb.shape), const(wc.shape), const(lam.shape), const(dskip.shape),
            const(wglu.shape), const(bglu.shape), const(g.shape),
        ],
        out_specs=pl.BlockSpec((r, width), lambda i: (i, 0)),
        out_shape=jax.ShapeDtypeStruct((rows_total, width), BF16),
        scratch_shapes=[
            pltpu.VMEM((SSM_LANE_GROUPS, r, wb.shape[2]), F32),
            pltpu.VMEM((SSM_LANE_GROUPS, V7X_SUBLANES, wb.shape[2]), F32),
        ],
        compiler_params=pltpu.CompilerParams(
            dimension_semantics=("arbitrary",),
            vmem_limit_bytes=V7X_VMEM_LIMIT_BYTES),
        name="s5ssm",
    )(u2, wb, wc, lam, dskip, wglu, bglu, g)


def _ffn_kernel(x_ref, att_ref, ssm_ref, woa_ref, wos_ref, gf_ref, wg_ref, wu_ref, wd_ref,
                gl_ref, o_ref):
    x1 = x_ref[0] + _dot(att_ref[0], woa_ref[...]) + _dot(ssm_ref[...], wos_ref[...])
    h = _rms(x1, gf_ref[...]).astype(BF16)
    a = (jax.nn.silu(_dot(h, wg_ref[...])) * _dot(h, wu_ref[...])).astype(BF16)
    o_ref[0] = _rms(x1 + _dot(a, wd_ref[...]), gl_ref[...])


def _ffn(x, att, ssm2, woa, wos, gf, wg, wu, wd, gl, *, tm):
    bsz, seq, d = x.shape
    const = lambda shape: pl.BlockSpec(shape, lambda b, s: (0,) * len(shape),
                                       pipeline_mode=pl.Buffered(1))
    return pl.pallas_call(
        _ffn_kernel,
        grid=(bsz, seq // tm),
        in_specs=[
            pl.BlockSpec((1, tm, d), lambda b, s: (b, s, 0)),
            pl.BlockSpec((1, tm, ATT_WIDTH), lambda b, s: (b, s, 0)),
            pl.BlockSpec((tm, SSM_WIDTH), lambda b, s: (s, b)),
            const(woa.shape), const(wos.shape), const(gf.shape),
            const(wg.shape), const(wu.shape), const(wd.shape), const(gl.shape),
        ],
        out_specs=pl.BlockSpec((1, tm, d), lambda b, s: (b, s, 0)),
        out_shape=jax.ShapeDtypeStruct((bsz, seq, d), x.dtype),
        compiler_params=pltpu.CompilerParams(
            dimension_semantics=("arbitrary", "arbitrary"),
            vmem_limit_bytes=V7X_VMEM_LIMIT_BYTES),
        name="outproj_ffn",
    )(x, att, ssm2, woa, wos, gf, wg, wu, wd, gl)


def _t5_bucket(n):
    max_exact = REL_BUCKETS // 2
    nf = jnp.maximum(n, 1).astype(F32)
    large = max_exact + (jnp.log(nf / max_exact) / math.log(REL_MAX_DIST / max_exact)
                         * (REL_BUCKETS - max_exact)).astype(jnp.int32)
    large = jnp.minimum(large, REL_BUCKETS - 1)
    return jnp.where(n < max_exact, n, large)


def _bias_tiles(rel_bias, tq, tk):
    assert tk >= REL_MAX_DIST
    ratio = tq // tk
    nd = tq + tk + 1
    neg = ratio * tk
    table = rel_bias.astype(F32)
    by_dist = table[_t5_bucket(jnp.arange(nd))]
    far = table[_t5_bucket(jnp.full((), REL_MAX_DIST, jnp.int32))]
    by_dist = (by_dist - far[None, :]) * LOG2E
    by_dist = jnp.concatenate([jnp.full((neg, ATT_HEADS), MASK_VALUE, F32), by_dist], axis=0).T
    m = tk + tq
    tiles = []
    for r in range(ratio + 1):
        start = neg - (tk - 1) - (r - 1) * tk
        c = by_dist[:, start:start + m]
        t = jnp.broadcast_to(c[:, None, :], (ATT_HEADS, tk, m)).reshape(ATT_HEADS, tk * m)
        t = t[:, :tk * (m - 1)].reshape(ATT_HEADS, tk, m - 1)
        tiles.append(t[:, :, tk - 1:tk - 1 + tq])
    return jnp.stack(tiles, axis=1)


def _ssm_params(A_re, A_im, log_dt, B_re, B_im, C_re, C_im):
    lam = lax.complex(A_re.astype(F32), A_im.astype(F32))
    dt = jnp.exp(log_dt.astype(F32))[:, None]
    lam_bar = jnp.exp(lam * dt)
    b_bar = ((lam_bar - 1.0) / lam)[:, :, None] * lax.complex(B_re.astype(F32), B_im.astype(F32))
    nlg = SSM_LANE_GROUPS
    gpl = SSM_GROUPS // nlg
    eye = jnp.eye(gpl, dtype=F32)

    def in_blocks(w):
        w = w.reshape(nlg, gpl, SSM_STATE, SSM_GROUP)
        return jnp.einsum('kgnc,gh->kgchn', w, eye).reshape(nlg, gpl * SSM_GROUP, gpl * SSM_STATE)

    def out_blocks(w):
        w = w.reshape(nlg, gpl, SSM_GROUP, SSM_STATE)
        return jnp.einsum('kgcn,gh->kgnhc', w, eye).reshape(nlg, gpl * SSM_STATE, gpl * SSM_GROUP)

    wb = jnp.concatenate([in_blocks(b_bar.real), in_blocks(b_bar.imag)], axis=2).astype(BF16)
    wc = jnp.concatenate([out_blocks(C_re.astype(F32)), out_blocks(-C_im.astype(F32))],
                         axis=1).astype(BF16)
    lam_ri = jnp.concatenate([lam_bar.real.reshape(nlg, 1, -1), lam_bar.imag.reshape(nlg, 1, -1)],
                             axis=2)
    lam_ri = jnp.broadcast_to(lam_ri, (nlg, V7X_SUBLANES, lam_ri.shape[2]))
    return wb, wc, lam_ri


def kernel(x, norm_mix_g, w_in, lambda_q1, lambda_k1, lambda_q2, lambda_k2, subln_g, rel_bias,
           A_re, A_im, log_dt, B_re, B_im, C_re, C_im, D_skip, w_glu, b_glu, ssm_norm_g, w_out,
           norm_ffn_g, w_gate, w_up, w_down, norm_final_g):
    bsz, seq, d = x.shape
    assert d == D_MODEL and bsz == V7X_SUBLANES
    ts, tq, tk, tm = PROJ_ROWS, ATT_TQ, ATT_TK, FFN_ROWS
    assert seq % ts == 0 and seq % tq == 0 and seq % tm == 0 and seq % SSM_STEPS == 0
    assert ts % tk == 0 and tq % tk == 0
    l = 0
    row = lambda v: v.astype(F32).reshape(1, -1)

    w = w_in[l].astype(F32)
    qk_scale = ATT_QK_DIM ** -0.5 * LOG2E
    wqT = (w[:, 0:QK_COLS] * qk_scale).T.astype(BF16)
    wk = w[:, QK_COLS:2 * QK_COLS].astype(BF16)
    wvT = w[:, 2 * QK_COLS:2 * QK_COLS + ATT_WIDTH].T.astype(BF16)
    wu = w[:, 2 * QK_COLS + ATT_WIDTH:].astype(BF16)
    lam_init = 0.8 - 0.6 * math.exp(-0.3 * l)
    lam = (jnp.exp(jnp.sum(lambda_q1[l].astype(F32) * lambda_k1[l].astype(F32)))
           - jnp.exp(jnp.sum(lambda_q2[l].astype(F32) * lambda_k2[l].astype(F32)))
           + lam_init).reshape(1, 1)
    bias = _bias_tiles(rel_bias, tq, tk)
    wb, wc, lam_ri = _ssm_params(A_re[l], A_im[l], log_dt[l], B_re[l], B_im[l], C_re[l], C_im[l])

    qT, k, vT, u = _inproj(x, row(norm_mix_g[l]), wqT, wk, wvT, wu, ts=ts, tk=tk)
    att = _attention(lam, qT, k, vT, bias, subln_g[l].astype(F32).reshape(-1, 1),
                     tq=tq, tk=tk, out_scale=1.0 - lam_init)
    ssm = _ssm(u.reshape(seq * bsz, SSM_WIDTH), wb, wc, lam_ri, row(D_skip[l]),
               w_glu[l].astype(BF16), row(b_glu[l]), row(ssm_norm_g[l]), steps=SSM_STEPS)

    wo = w_out[l].astype(BF16)
    return _ffn(x, att, ssm.reshape(seq, bsz * SSM_WIDTH), wo[0:ATT_WIDTH], wo[ATT_WIDTH:],
                row(norm_ffn_g[l]), w_gate[l].astype(BF16), w_up[l].astype(BF16),
                w_down[l].astype(BF16), row(norm_final_g), tm=tm)
```

```python
import functools
import math

import jax
import jax.numpy as jnp
import numpy as np
from jax import lax
from jax.experimental import pallas as pl
from jax.experimental.pallas import tpu as pltpu

F32 = jnp.float32
BF16 = jnp.bfloat16

D_MODEL = 1024
ATT_WIDTH = 512
SSM_WIDTH = 512
ATT_HEADS = 4
ATT_QK_DIM = 64
ATT_V_DIM = 128
ATT_V_ROWS = ATT_V_DIM + 16
QK_COLS = 512
SSM_GROUP = 16
SSM_GROUPS = 32
SSM_STATE = 64
REL_BUCKETS = 32
REL_MAX_DIST = 128
EPS = 1e-6
LOG2E = math.log2(math.e)
MASK_VALUE = -1e30

V7X_SUBLANES = 8
V7X_LANES = 128
V7X_VMEM_LIMIT_BYTES = 56 * 1024 * 1024

PROJ_ROWS = 512
ATT_TQ = 512
ATT_TK = 512
ATT_COLS = 256
SSM_STEPS = 64
SSM_LANE_GROUPS = 4
FFN_ROWS = 512

_NT = (((1,), (1,)), ((), ()))


def _dot(a, b):
    return jnp.dot(a, b, preferred_element_type=F32)


def _rms(x, g):
    ms = jnp.mean(x * x, axis=-1, keepdims=True)
    return x * lax.rsqrt(ms + EPS) * g


def _inproj_kernel(x_ref, g_ref, wqT_ref, wk_ref, wvT_ref, wu_ref,
                   qT_ref, k_ref, vT_ref, u_ref, *, tk):
    h = _rms(x_ref[0], g_ref[...]).astype(BF16)
    qT = lax.dot_general(wqT_ref[...], h, _NT, preferred_element_type=F32)
    qT_ref[0] = qT.astype(BF16)
    k_ref[0] = _dot(h, wk_ref[...]).astype(BF16)
    vT = lax.dot_general(wvT_ref[...], h, _NT, preferred_element_type=F32).astype(BF16)
    ones = jnp.ones((ATT_V_ROWS - ATT_V_DIM, tk), BF16)
    for c in range(vT_ref.shape[1]):
        for hd in range(ATT_HEADS):
            r0 = hd * ATT_V_ROWS
            vT_ref[0, c, r0:r0 + ATT_V_DIM, :] = vT[hd * ATT_V_DIM:(hd + 1) * ATT_V_DIM,
                                                    c * tk:(c + 1) * tk]
            vT_ref[0, c, r0 + ATT_V_DIM:r0 + ATT_V_ROWS, :] = ones
    u_ref[0] = _dot(h, wu_ref[...])


def _inproj(x, g, wqT, wk, wvT, wu, *, ts, tk):
    bsz, seq, d = x.shape
    ns = seq // ts
    const = lambda shape: pl.BlockSpec(shape, lambda b, s: (0,) * len(shape),
                                       pipeline_mode=pl.Buffered(1))
    return pl.pallas_call(
        functools.partial(_inproj_kernel, tk=tk),
        grid=(bsz, ns),
        in_specs=[
            pl.BlockSpec((1, ts, d), lambda b, s: (b, s, 0)),
            const((1, d)),
            const((QK_COLS, d)),
            const((d, QK_COLS)),
            const((ATT_WIDTH, d)),
            const((d, SSM_WIDTH)),
        ],
        out_specs=[
            pl.BlockSpec((1, QK_COLS, ts), lambda b, s: (b, 0, s)),
            pl.BlockSpec((1, ts, QK_COLS), lambda b, s: (b, s, 0)),
            pl.BlockSpec((1, ts // tk, ATT_HEADS * ATT_V_ROWS, tk), lambda b, s: (b, s, 0, 0)),
            pl.BlockSpec((1, ts, SSM_WIDTH), lambda b, s: (b, s, 0)),
        ],
        out_shape=[
            jax.ShapeDtypeStruct((bsz, QK_COLS, seq), BF16),
            jax.ShapeDtypeStruct((bsz, seq, QK_COLS), BF16),
            jax.ShapeDtypeStruct((bsz, seq // tk, ATT_HEADS * ATT_V_ROWS, tk), BF16),
            jax.ShapeDtypeStruct((bsz, seq, SSM_WIDTH), F32),
        ],
        compiler_params=pltpu.CompilerParams(
            dimension_semantics=("arbitrary", "arbitrary"),
            vmem_limit_bytes=V7X_VMEM_LIMIT_BYTES),
        name="inproj",
    )(x, g, wqT, wk, wvT, wu)


def _attn_kernel(lam_ref, qT_ref, k_ref, vT_ref, bias_ref, g_ref, o_ref,
                 qbd, s_a, s_b, m_s, acc_s, *, tq, tk, out_scale):
    qi = pl.program_id(2)
    ratio = tq // tk
    dk = ATT_QK_DIM
    bufs = (s_a, s_b)

    qbd[...] = jnp.zeros_like(qbd)
    qbd[0:dk, 0:tq] = qT_ref[0, 0:dk, :]
    qbd[dk:2 * dk, tq:2 * tq] = qT_ref[0, dk:2 * dk, :]
    m_s[...] = jnp.full_like(m_s, -jnp.inf)
    acc_s[...] = jnp.zeros_like(acc_s)

    def live_rows(r, c0):
        if r is None or r == 0:
            return tk
        return max(0, min(tk, c0 % tq + ATT_COLS - (r - 1) * tk))

    def logits(tile, buf):
        j, r = tile
        r0 = pl.multiple_of(j * tk, tk)
        if r is None or r == 0:
            buf[...] = _dot(k_ref[0, pl.ds(r0, tk), :], qbd[...])
            return
        for c0 in range(0, 2 * tq, ATT_COLS):
            rows = live_rows(r, c0)
            if rows:
                buf[0:rows, c0:c0 + ATT_COLS] = _dot(k_ref[0, pl.ds(r0, rows), :],
                                                     qbd[:, c0:c0 + ATT_COLS])

    def softmax_pv(tile, buf):
        j, r = tile
        for c0 in range(0, 2 * tq, ATT_COLS):
            rows = live_rows(r, c0)
            if rows == 0:
                continue
            cols = slice(c0, c0 + ATT_COLS)
            s = buf[0:rows, cols]
            if r is not None:
                s = s + bias_ref[0, r, 0:rows, c0 % tq:c0 % tq + ATT_COLS]
            m_old = m_s[:, cols]
            m_new = jnp.maximum(m_old, jnp.max(s, axis=0, keepdims=True))
            alpha = jnp.exp2(m_old - m_new)
            p = jnp.exp2(s - m_new)
            acc_s[:, cols] = alpha * acc_s[:, cols] + _dot(vT_ref[0, j, :, 0:rows],
                                                           p.astype(BF16))
            m_s[:, cols] = m_new

    def run_tiles(tiles, preloaded):
        if not preloaded:
            logits(tiles[0], bufs[0])
        for i, tile in enumerate(tiles):
            if i + 1 < len(tiles):
                logits(tiles[i + 1], bufs[(i + 1) % 2])
            softmax_pv(tile, bufs[i % 2])

    diag = [(qi * ratio + (r - 1), r) for r in range(1, ratio + 1)]

    @pl.when(qi == 0)
    def _():
        run_tiles(diag, preloaded=False)

    @pl.when(qi > 0)
    def _():
        n_pre = qi * ratio
        logits((0, None), s_a)

        def pair(p, carry):
            logits((2 * p + 1, None), s_b)
            softmax_pv((2 * p, None), s_a)
            logits((2 * p + 2, None), s_a)
            softmax_pv((2 * p + 1, None), s_b)
            return carry

        npairs = (n_pre - 1) // 2
        lax.fori_loop(0, npairs, pair, 0)
        t = 2 * npairs

        @pl.when(n_pre - t == 1)
        def _():
            run_tiles([(t, 0)] + diag, preloaded=True)

        @pl.when(n_pre - t == 2)
        def _():
            run_tiles([(t, None), (t + 1, 0)] + diag, preloaded=True)

    lam = lam_ref[0, 0]
    acc = acc_s[0:ATT_V_DIM, :]
    l = acc_s[ATT_V_DIM:ATT_V_DIM + 1, :]
    o = acc[:, 0:tq] / l[:, 0:tq] - lam * (acc[:, tq:2 * tq] / l[:, tq:2 * tq])
    ms = jnp.mean(o * o, axis=0, keepdims=True)
    o = o * lax.rsqrt(ms + EPS) * g_ref[...] * out_scale
    o_ref[0] = o.T.astype(BF16)


def _attention(lam, qT, k, vT, bias, g, *, tq, tk, out_scale):
    bsz, seq, _ = k.shape
    nq = seq // tq
    ratio = tq // tk
    return pl.pallas_call(
        functools.partial(_attn_kernel, tq=tq, tk=tk, out_scale=out_scale),
        grid=(bsz, ATT_HEADS, nq),
        in_specs=[
            pl.BlockSpec(memory_space=pltpu.SMEM),
            pl.BlockSpec((1, 2 * ATT_QK_DIM, tq), lambda b, h, q: (b, h, q)),
            pl.BlockSpec((1, seq, 2 * ATT_QK_DIM), lambda b, h, q: (b, 0, h)),
            pl.BlockSpec((1, seq // tk, ATT_V_ROWS, tk), lambda b, h, q: (b, 0, h, 0)),
            pl.BlockSpec((1, ratio + 1, tk, tq), lambda b, h, q: (h, 0, 0, 0)),
            pl.BlockSpec((ATT_V_DIM, 1), lambda b, h, q: (0, 0)),
        ],
        out_specs=pl.BlockSpec((1, tq, ATT_V_DIM), lambda b, h, q: (b, q, h)),
        out_shape=jax.ShapeDtypeStruct((bsz, seq, ATT_WIDTH), BF16),
        scratch_shapes=[
            pltpu.VMEM((2 * ATT_QK_DIM, 2 * tq), BF16),
            pltpu.VMEM((tk, 2 * tq), F32),
            pltpu.VMEM((tk, 2 * tq), F32),
            pltpu.VMEM((1, 2 * tq), F32),
            pltpu.VMEM((ATT_V_ROWS, 2 * tq), F32),
        ],
        compiler_params=pltpu.CompilerParams(
            dimension_semantics=("arbitrary", "arbitrary", "arbitrary"),
            vmem_limit_bytes=V7X_VMEM_LIMIT_BYTES),
        name="diffattn",
    )(lam, qT, k, vT, bias, g)


def _ssm_kernel(u_ref, wb_ref, wc_ref, lam_ref, d_ref, wglu_ref, bglu_ref, g_ref,
                o_ref, uin, res, xs, st, *, steps):
    rows = V7X_SUBLANES
    half = xs.shape[2] // 2
    lanes = V7X_LANES
    bsz = u_ref.shape[0]

    @pl.when(pl.program_id(0) == 0)
    def _():
        st[...] = jnp.zeros_like(st)

    for b in range(bsz):
        for c in range(SSM_LANE_GROUPS):
            uin[c, pl.ds(b, steps, stride=bsz), :] = u_ref[b, :, c * lanes:(c + 1) * lanes]

    ys = []
    for c in range(SSM_LANE_GROUPS):
        xs[c] = _dot(uin[c].astype(BF16), wb_ref[c])
        lre = lam_ref[c, :, 0:half]
        lim = lam_ref[c, :, half:2 * half]

        def step(t, carry, c=c, lre=lre, lim=lim):
            sre, sim = carry
            r = pl.multiple_of(t * rows, rows)
            bre = xs[c, pl.ds(r, rows), 0:half]
            bim = xs[c, pl.ds(r, rows), half:2 * half]
            nre = lre * sre - lim * sim + bre
            nim = lre * sim + lim * sre + bim
            xs[c, pl.ds(r, rows), 0:half] = nre
            xs[c, pl.ds(r, rows), half:2 * half] = nim
            return nre, nim

        sre, sim = lax.fori_loop(0, steps, step,
                                 (st[c, :, 0:half], st[c, :, half:2 * half]), unroll=True)
        st[c, :, 0:half] = sre
        st[c, :, half:2 * half] = sim
        ys.append(_dot(xs[c].astype(BF16), wc_ref[c]))

    u = jnp.concatenate([uin[c] for c in range(SSM_LANE_GROUPS)], axis=1)
    y = jnp.concatenate(ys, axis=1) + d_ref[...] * u
    gl = jax.nn.gelu(y)
    z = _dot(gl.astype(BF16), wglu_ref[...]) + bglu_ref[...]
    out = _rms(gl * jax.nn.sigmoid(z), g_ref[...])
    for c in range(SSM_LANE_GROUPS):
        res[c] = out[:, c * lanes:(c + 1) * lanes]
    for b in range(bsz):
        for c in range(SSM_LANE_GROUPS):
            o_ref[b, :, c * lanes:(c + 1) * lanes] = (
                res[c, pl.ds(b, steps, stride=bsz), :].astype(BF16))


def _ssm(u, wb, wc, lam, dskip, wglu, bglu, g, *, steps):
    bsz, seq, width = u.shape
    r = steps * bsz
    const = lambda shape: pl.BlockSpec(shape, lambda i: (0,) * len(shape),
                                       pipeline_mode=pl.Buffered(1))
    return pl.pallas_call(
        functools.partial(_ssm_kernel, steps=steps),
        grid=(seq // steps,),
        in_specs=[
            pl.BlockSpec((bsz, steps, width), lambda i: (0, i, 0)),
            const(wb.shape), const(wc.shape), const(lam.shape), const(dskip.shape),
            const(wglu.shape), const(bglu.shape), const(g.shape),
        ],
        out_specs=pl.BlockSpec((bsz, steps, width), lambda i: (0, i, 0)),
        out_shape=jax.ShapeDtypeStruct((bsz, seq, width), BF16),
        scratch_shapes=[
            pltpu.VMEM((SSM_LANE_GROUPS, r, V7X_LANES), F32),
            pltpu.VMEM((SSM_LANE_GROUPS, r, V7X_LANES), F32),
            pltpu.VMEM((SSM_LANE_GROUPS, r, wb.shape[2]), F32),
            pltpu.VMEM((SSM_LANE_GROUPS, V7X_SUBLANES, wb.shape[2]), F32),
        ],
        compiler_params=pltpu.CompilerParams(
            dimension_semantics=("arbitrary",),
            vmem_limit_bytes=V7X_VMEM_LIMIT_BYTES),
        name="s5ssm",
    )(u, wb, wc, lam, dskip, wglu, bglu, g)


def _ffn_kernel(x_ref, att_ref, ssm_ref, woa_ref, wos_ref, gf_ref, wg_ref, wu_ref, wd_ref,
                gl_ref, o_ref):
    x1 = x_ref[0] + _dot(att_ref[0], woa_ref[...]) + _dot(ssm_ref[0], wos_ref[...])
    h = _rms(x1, gf_ref[...]).astype(BF16)
    a = (jax.nn.silu(_dot(h, wg_ref[...])) * _dot(h, wu_ref[...])).astype(BF16)
    o_ref[0] = _rms(x1 + _dot(a, wd_ref[...]), gl_ref[...])


def _ffn(x, att, ssm2, woa, wos, gf, wg, wu, wd, gl, *, tm):
    bsz, seq, d = x.shape
    const = lambda shape: pl.BlockSpec(shape, lambda b, s: (0,) * len(shape),
                                       pipeline_mode=pl.Buffered(1))
    return pl.pallas_call(
        _ffn_kernel,
        grid=(bsz, seq // tm),
        in_specs=[
            pl.BlockSpec((1, tm, d), lambda b, s: (b, s, 0)),
            pl.BlockSpec((1, tm, ATT_WIDTH), lambda b, s: (b, s, 0)),
            pl.BlockSpec((1, tm, SSM_WIDTH), lambda b, s: (b, s, 0)),
            const(woa.shape), const(wos.shape), const(gf.shape),
            const(wg.shape), const(wu.shape), const(wd.shape), const(gl.shape),
        ],
        out_specs=pl.BlockSpec((1, tm, d), lambda b, s: (b, s, 0)),
        out_shape=jax.ShapeDtypeStruct((bsz, seq, d), x.dtype),
        compiler_params=pltpu.CompilerParams(
            dimension_semantics=("arbitrary", "arbitrary"),
            vmem_limit_bytes=V7X_VMEM_LIMIT_BYTES),
        name="outproj_ffn",
    )(x, att, ssm2, woa, wos, gf, wg, wu, wd, gl)


def _t5_bucket(n):
    max_exact = REL_BUCKETS // 2
    nf = jnp.maximum(n, 1).astype(F32)
    large = max_exact + (jnp.log(nf / max_exact) / math.log(REL_MAX_DIST / max_exact)
                         * (REL_BUCKETS - max_exact)).astype(jnp.int32)
    large = jnp.minimum(large, REL_BUCKETS - 1)
    return jnp.where(n < max_exact, n, large)


def _bias_tiles(rel_bias, tq, tk):
    assert tk >= REL_MAX_DIST
    ratio = tq // tk
    nd = tq + tk + 1
    neg = ratio * tk
    table = rel_bias.astype(F32)
    by_dist = table[_t5_bucket(jnp.arange(nd))]
    far = table[_t5_bucket(jnp.full((), REL_MAX_DIST, jnp.int32))]
    by_dist = (by_dist - far[None, :]) * LOG2E
    by_dist = jnp.concatenate([jnp.full((neg, ATT_HEADS), MASK_VALUE, F32), by_dist], axis=0).T
    m = tk + tq
    tiles = []
    for r in range(ratio + 1):
        start = neg - (tk - 1) - (r - 1) * tk
        c = by_dist[:, start:start + m]
        t = jnp.broadcast_to(c[:, None, :], (ATT_HEADS, tk, m)).reshape(ATT_HEADS, tk * m)
        t = t[:, :tk * (m - 1)].reshape(ATT_HEADS, tk, m - 1)
        tiles.append(t[:, :, tk - 1:tk - 1 + tq])
    return jnp.stack(tiles, axis=1)


def _ssm_params(A_re, A_im, log_dt, B_re, B_im, C_re, C_im):
    lam = lax.complex(A_re.astype(F32), A_im.astype(F32))
    dt = jnp.exp(log_dt.astype(F32))[:, None]
    lam_bar = jnp.exp(lam * dt)
    b_bar = ((lam_bar - 1.0) / lam)[:, :, None] * lax.complex(B_re.astype(F32), B_im.astype(F32))
    nlg = SSM_LANE_GROUPS
    gpl = SSM_GROUPS // nlg
    eye = jnp.eye(gpl, dtype=F32)

    def in_blocks(w):
        w = w.reshape(nlg, gpl, SSM_STATE, SSM_GROUP)
        return jnp.einsum('kgnc,gh->kgchn', w, eye).reshape(nlg, gpl * SSM_GROUP, gpl * SSM_STATE)

    def out_blocks(w):
        w = w.reshape(nlg, gpl, SSM_GROUP, SSM_STATE)
        return jnp.einsum('kgcn,gh->kgnhc', w, eye).reshape(nlg, gpl * SSM_STATE, gpl * SSM_GROUP)

    wb = jnp.concatenate([in_blocks(b_bar.real), in_blocks(b_bar.imag)], axis=2).astype(BF16)
    wc = jnp.concatenate([out_blocks(C_re.astype(F32)), out_blocks(-C_im.astype(F32))],
                         axis=1).astype(BF16)
    lam_ri = jnp.concatenate([lam_bar.real.reshape(nlg, 1, -1), lam_bar.imag.reshape(nlg, 1, -1)],
                             axis=2)
    lam_ri = jnp.broadcast_to(lam_ri, (nlg, V7X_SUBLANES, lam_ri.shape[2]))
    return wb, wc, lam_ri


def kernel(x, norm_mix_g, w_in, lambda_q1, lambda_k1, lambda_q2, lambda_k2, subln_g, rel_bias,
           A_re, A_im, log_dt, B_re, B_im, C_re, C_im, D_skip, w_glu, b_glu, ssm_norm_g, w_out,
           norm_ffn_g, w_gate, w_up, w_down, norm_final_g):
    bsz, seq, d = x.shape
    assert d == D_MODEL and bsz == V7X_SUBLANES
    ts, tq, tk, tm = PROJ_ROWS, ATT_TQ, ATT_TK, FFN_ROWS
    assert seq % ts == 0 and seq % tq == 0 and seq % tm == 0 and seq % SSM_STEPS == 0
    assert ts % tk == 0 and tq % tk == 0
    l = 0
    row = lambda v: v.astype(F32).reshape(1, -1)

    w = w_in[l].astype(F32)
    qk_scale = ATT_QK_DIM ** -0.5 * LOG2E
    wqT = (w[:, 0:QK_COLS] * qk_scale).T.astype(BF16)
    wk = w[:, QK_COLS:2 * QK_COLS].astype(BF16)
    wvT = w[:, 2 * QK_COLS:2 * QK_COLS + ATT_WIDTH].T.astype(BF16)
    wu = w[:, 2 * QK_COLS + ATT_WIDTH:].astype(BF16)
    lam_init = 0.8 - 0.6 * math.exp(-0.3 * l)
    lam = (jnp.exp(jnp.sum(lambda_q1[l].astype(F32) * lambda_k1[l].astype(F32)))
           - jnp.exp(jnp.sum(lambda_q2[l].astype(F32) * lambda_k2[l].astype(F32)))
           + lam_init).reshape(1, 1)
    bias = _bias_tiles(rel_bias, tq, tk)
    wb, wc, lam_ri = _ssm_params(A_re[l], A_im[l], log_dt[l], B_re[l], B_im[l], C_re[l], C_im[l])

    qT, k, vT, u = _inproj(x, row(norm_mix_g[l]), wqT, wk, wvT, wu, ts=ts, tk=tk)
    att = _attention(lam, qT, k, vT, bias, subln_g[l].astype(F32).reshape(-1, 1),
                     tq=tq, tk=tk, out_scale=1.0 - lam_init)
    ssm = _ssm(u, wb, wc, lam_ri, row(D_skip[l]),
               w_glu[l].astype(BF16), row(b_glu[l]), row(ssm_norm_g[l]), steps=SSM_STEPS)

    wo = w_out[l].astype(BF16)
    return _ffn(x, att, ssm, wo[0:ATT_WIDTH], wo[ATT_WIDTH:],
                row(norm_ffn_g[l]), w_gate[l].astype(BF16), w_up[l].astype(BF16),
                w_down[l].astype(BF16), row(norm_final_g), tm=tm)
```

```python
import functools
import math

import jax
import jax.numpy as jnp
import numpy as np
from jax import lax
from jax.experimental import pallas as pl
from jax.experimental.pallas import tpu as pltpu

F32 = jnp.float32
BF16 = jnp.bfloat16

D_MODEL = 1024
ATT_WIDTH = 512
SSM_WIDTH = 512
ATT_HEADS = 4
ATT_QK_DIM = 64
ATT_V_DIM = 128
QK_COLS = 512
SSM_GROUP = 16
SSM_GROUPS = 32
SSM_STATE = 64
REL_BUCKETS = 32
REL_MAX_DIST = 128
EPS = 1e-6
LOG2E = math.log2(math.e)
MASK_VALUE = -1e30

V7X_SUBLANES = 8
V7X_LANES = 128
V7X_VMEM_LIMIT_BYTES = 56 * 1024 * 1024

PROJ_ROWS = 512
ATT_TQ = 512
ATT_TK = 512
ATT_COLS = 256
SSM_STEPS = 64
SSM_LANE_GROUPS = 4
FFN_ROWS = 512

_NT = (((1,), (1,)), ((), ()))


def _dot(a, b):
    return jnp.dot(a, b, preferred_element_type=F32)


def _rms(x, g):
    ms = jnp.mean(x * x, axis=-1, keepdims=True)
    return x * lax.rsqrt(ms + EPS) * g


def _inproj_kernel(x_ref, g_ref, wqT_ref, wk_ref, wvT_ref, wu_ref,
                   qT_ref, k_ref, vT_ref, u_ref, *, tk):
    h = _rms(x_ref[0], g_ref[...]).astype(BF16)
    qT = lax.dot_general(wqT_ref[...], h, _NT, preferred_element_type=F32)
    qT_ref[0] = qT.astype(BF16)
    k_ref[0] = _dot(h, wk_ref[...]).astype(BF16)
    vT = lax.dot_general(wvT_ref[...], h, _NT, preferred_element_type=F32).astype(BF16)
    for c in range(vT_ref.shape[1]):
        vT_ref[0, c] = vT[:, c * tk:(c + 1) * tk]
    u_ref[0] = _dot(h, wu_ref[...])


def _inproj(x, g, wqT, wk, wvT, wu, *, ts, tk):
    bsz, seq, d = x.shape
    ns = seq // ts
    const = lambda shape: pl.BlockSpec(shape, lambda b, s: (0,) * len(shape),
                                       pipeline_mode=pl.Buffered(1))
    return pl.pallas_call(
        functools.partial(_inproj_kernel, tk=tk),
        grid=(bsz, ns),
        in_specs=[
            pl.BlockSpec((1, ts, d), lambda b, s: (b, s, 0)),
            const((1, d)),
            const((QK_COLS, d)),
            const((d, QK_COLS)),
            const((ATT_WIDTH, d)),
            const((d, SSM_WIDTH)),
        ],
        out_specs=[
            pl.BlockSpec((1, QK_COLS, ts), lambda b, s: (b, 0, s)),
            pl.BlockSpec((1, ts, QK_COLS), lambda b, s: (b, s, 0)),
            pl.BlockSpec((1, ts // tk, ATT_WIDTH, tk), lambda b, s: (b, s, 0, 0)),
            pl.BlockSpec((1, ts, SSM_WIDTH), lambda b, s: (b, s, 0)),
        ],
        out_shape=[
            jax.ShapeDtypeStruct((bsz, QK_COLS, seq), BF16),
            jax.ShapeDtypeStruct((bsz, seq, QK_COLS), BF16),
            jax.ShapeDtypeStruct((bsz, seq // tk, ATT_WIDTH, tk), BF16),
            jax.ShapeDtypeStruct((bsz, seq, SSM_WIDTH), F32),
        ],
        compiler_params=pltpu.CompilerParams(
            dimension_semantics=("arbitrary", "arbitrary"),
            vmem_limit_bytes=V7X_VMEM_LIMIT_BYTES),
        name="inproj",
    )(x, g, wqT, wk, wvT, wu)


def _attn_kernel(lam_ref, qT_ref, k_ref, vT_ref, bias_ref, g_ref, o_ref,
                 qbd, s_a, s_b, mx_a, mx_b, m_s, l_s, acc_s, *, tq, tk, out_scale):
    qi = pl.program_id(2)
    ratio = tq // tk
    dk = ATT_QK_DIM
    bufs = ((s_a, mx_a), (s_b, mx_b))

    qbd[...] = jnp.zeros_like(qbd)
    qbd[0:dk, 0:tq] = qT_ref[0, 0:dk, :]
    qbd[dk:2 * dk, tq:2 * tq] = qT_ref[0, dk:2 * dk, :]
    m_s[...] = jnp.full_like(m_s, -jnp.inf)
    l_s[...] = jnp.zeros_like(l_s)
    acc_s[...] = jnp.zeros_like(acc_s)

    def live_rows(r, c0):
        if r is None or r == 0:
            return tk
        return max(0, min(tk, c0 % tq + ATT_COLS - (r - 1) * tk))

    def logits(tile, bufs2):
        j, r = tile
        buf, mx = bufs2
        r0 = pl.multiple_of(j * tk, tk)
        if r is None:
            s = _dot(k_ref[0, pl.ds(r0, tk), :], qbd[...])
            for c0 in range(0, 2 * tq, ATT_COLS):
                buf[c0 // ATT_COLS] = s[:, c0:c0 + ATT_COLS]
            mx[...] = jnp.max(s, axis=0, keepdims=True)
            return
        for c0 in range(0, 2 * tq, ATT_COLS):
            rows = live_rows(r, c0)
            if rows:
                cols = slice(c0, c0 + ATT_COLS)
                s = _dot(k_ref[0, pl.ds(r0, rows), :], qbd[:, cols])
                s = s + bias_ref[0, r, 0:rows, c0 % tq:c0 % tq + ATT_COLS]
                buf[c0 // ATT_COLS, 0:rows, :] = s
                mx[:, cols] = jnp.max(s, axis=0, keepdims=True)

    def softmax_pv(tile, bufs2):
        j, r = tile
        buf, mx = bufs2
        for c0 in range(0, 2 * tq, ATT_COLS):
            rows = live_rows(r, c0)
            if rows == 0:
                continue
            cols = slice(c0, c0 + ATT_COLS)
            m_old = m_s[:, cols]
            m_new = jnp.maximum(m_old, mx[:, cols])
            alpha = jnp.exp2(m_old - m_new)
            p = jnp.exp2(buf[c0 // ATT_COLS, 0:rows, :] - m_new)
            l_s[:, cols] = alpha * l_s[:, cols] + jnp.sum(p, axis=0, keepdims=True)
            acc_s[:, cols] = alpha * acc_s[:, cols] + _dot(vT_ref[0, j, :, 0:rows],
                                                           p.astype(BF16))
            m_s[:, cols] = m_new

    def run_tiles(tiles, preloaded):
        if not preloaded:
            logits(tiles[0], bufs[0])
        for i, tile in enumerate(tiles):
            if i + 1 < len(tiles):
                logits(tiles[i + 1], bufs[(i + 1) % 2])
            softmax_pv(tile, bufs[i % 2])

    diag = [(qi * ratio + (r - 1), r) for r in range(1, ratio + 1)]

    @pl.when(qi == 0)
    def _():
        run_tiles(diag, preloaded=False)

    @pl.when(qi > 0)
    def _():
        n_pre = qi * ratio
        a, b = bufs

        def pair(p, carry):
            logits((2 * p + 1, None), b)
            softmax_pv((2 * p, None), a)
            logits((2 * p + 2, None), a)
            softmax_pv((2 * p + 1, None), b)
            return carry

        npairs = jnp.maximum(n_pre - 2, 0) // 2
        t = 2 * npairs

        @pl.when(n_pre == 1)
        def _():
            run_tiles([(0, 0)] + diag, preloaded=False)

        @pl.when(n_pre > 1)
        def _():
            logits((0, None), a)
            lax.fori_loop(0, npairs, pair, 0)

            @pl.when(n_pre - t == 2)
            def _():
                run_tiles([(t, None), (t + 1, 0)] + diag, preloaded=True)

            @pl.when(n_pre - t == 3)
            def _():
                run_tiles([(t, None), (t + 1, None), (t + 2, 0)] + diag, preloaded=True)

    lam = lam_ref[0, 0]
    acc = acc_s[...]
    l = l_s[...]
    o = acc[:, 0:tq] / l[:, 0:tq] - lam * (acc[:, tq:2 * tq] / l[:, tq:2 * tq])
    ms = jnp.mean(o * o, axis=0, keepdims=True)
    o = o * lax.rsqrt(ms + EPS) * g_ref[...] * out_scale
    o_ref[0] = o.T.astype(BF16)


def _attention(lam, qT, k, vT, bias, g, *, tq, tk, out_scale):
    bsz, seq, _ = k.shape
    nq = seq // tq
    ratio = tq // tk
    return pl.pallas_call(
        functools.partial(_attn_kernel, tq=tq, tk=tk, out_scale=out_scale),
        grid=(bsz, ATT_HEADS, nq),
        in_specs=[
            pl.BlockSpec(memory_space=pltpu.SMEM),
            pl.BlockSpec((1, 2 * ATT_QK_DIM, tq), lambda b, h, q: (b, h, q)),
            pl.BlockSpec((1, seq, 2 * ATT_QK_DIM), lambda b, h, q: (b, 0, h)),
            pl.BlockSpec((1, seq // tk, ATT_V_DIM, tk), lambda b, h, q: (b, 0, h, 0)),
            pl.BlockSpec((1, ratio + 1, tk, tq), lambda b, h, q: (h, 0, 0, 0)),
            pl.BlockSpec((ATT_V_DIM, 1), lambda b, h, q: (0, 0)),
        ],
        out_specs=pl.BlockSpec((1, tq, ATT_V_DIM), lambda b, h, q: (b, q, h)),
        out_shape=jax.ShapeDtypeStruct((bsz, seq, ATT_WIDTH), BF16),
        scratch_shapes=[
            pltpu.VMEM((2 * ATT_QK_DIM, 2 * tq), BF16),
            pltpu.VMEM((2 * tq // ATT_COLS, tk, ATT_COLS), F32),
            pltpu.VMEM((2 * tq // ATT_COLS, tk, ATT_COLS), F32),
            pltpu.VMEM((1, 2 * tq), F32),
            pltpu.VMEM((1, 2 * tq), F32),
            pltpu.VMEM((1, 2 * tq), F32),
            pltpu.VMEM((1, 2 * tq), F32),
            pltpu.VMEM((ATT_V_DIM, 2 * tq), F32),
        ],
        compiler_params=pltpu.CompilerParams(
            dimension_semantics=("arbitrary", "arbitrary", "arbitrary"),
            vmem_limit_bytes=V7X_VMEM_LIMIT_BYTES),
        name="diffattn",
    )(lam, qT, k, vT, bias, g)


def _ssm_kernel(u_ref, wb_ref, wc_ref, lam_ref, d_ref, wglu_ref, bglu_ref, g_ref,
                o_ref, uin, res, xs, st, *, steps):
    rows = V7X_SUBLANES
    half = xs.shape[2] // 2
    lanes = V7X_LANES
    bsz = u_ref.shape[0]

    @pl.when(pl.program_id(0) == 0)
    def _():
        st[...] = jnp.zeros_like(st)

    for b in range(bsz):
        for c in range(SSM_LANE_GROUPS):
            uin[c, pl.ds(b, steps, stride=bsz), :] = u_ref[b, :, c * lanes:(c + 1) * lanes]

    ys = []
    for c in range(SSM_LANE_GROUPS):
        xs[c] = _dot(uin[c].astype(BF16), wb_ref[c])
        lre = lam_ref[c, :, 0:half]
        lim = lam_ref[c, :, half:2 * half]

        def step(t, carry, c=c, lre=lre, lim=lim):
            sre, sim = carry
            r = pl.multiple_of(t * rows, rows)
            bre = xs[c, pl.ds(r, rows), 0:half]
            bim = xs[c, pl.ds(r, rows), half:2 * half]
            nre = lre * sre - lim * sim + bre
            nim = lre * sim + lim * sre + bim
            xs[c, pl.ds(r, rows), 0:half] = nre
            xs[c, pl.ds(r, rows), half:2 * half] = nim
            return nre, nim

        sre, sim = lax.fori_loop(0, steps, step,
                                 (st[c, :, 0:half], st[c, :, half:2 * half]), unroll=True)
        st[c, :, 0:half] = sre
        st[c, :, half:2 * half] = sim
        ys.append(_dot(xs[c].astype(BF16), wc_ref[c]))

    u = jnp.concatenate([uin[c] for c in range(SSM_LANE_GROUPS)], axis=1)
    y = jnp.concatenate(ys, axis=1) + d_ref[...] * u
    gl = jax.nn.gelu(y)
    z = _dot(gl.astype(BF16), wglu_ref[...]) + bglu_ref[...]
    out = _rms(gl * jax.nn.sigmoid(z), g_ref[...])
    for c in range(SSM_LANE_GROUPS):
        res[c] = out[:, c * lanes:(c + 1) * lanes]
    for b in range(bsz):
        for c in range(SSM_LANE_GROUPS):
            o_ref[b, :, c * lanes:(c + 1) * lanes] = (
                res[c, pl.ds(b, steps, stride=bsz), :].astype(BF16))


def _ssm(u, wb, wc, lam, dskip, wglu, bglu, g, *, steps):
    bsz, seq, width = u.shape
    r = steps * bsz
    const = lambda shape: pl.BlockSpec(shape, lambda i: (0,) * len(shape),
                                       pipeline_mode=pl.Buffered(1))
    return pl.pallas_call(
        functools.partial(_ssm_kernel, steps=steps),
        grid=(seq // steps,),
        in_specs=[
            pl.BlockSpec((bsz, steps, width), lambda i: (0, i, 0)),
            const(wb.shape), const(wc.shape), const(lam.shape), const(dskip.shape),
            const(wglu.shape), const(bglu.shape), const(g.shape),
        ],
        out_specs=pl.BlockSpec((bsz, steps, width), lambda i: (0, i, 0)),
        out_shape=jax.ShapeDtypeStruct((bsz, seq, width), BF16),
        scratch_shapes=[
            pltpu.VMEM((SSM_LANE_GROUPS, r, V7X_LANES), F32),
            pltpu.VMEM((SSM_LANE_GROUPS, r, V7X_LANES), F32),
            pltpu.VMEM((SSM_LANE_GROUPS, r, wb.shape[2]), F32),
            pltpu.VMEM((SSM_LANE_GROUPS, V7X_SUBLANES, wb.shape[2]), F32),
        ],
        compiler_params=pltpu.CompilerParams(
            dimension_semantics=("arbitrary",),
            vmem_limit_bytes=V7X_VMEM_LIMIT_BYTES),
        name="s5ssm",
    )(u, wb, wc, lam, dskip, wglu, bglu, g)


def _ffn_kernel(x_ref, att_ref, ssm_ref, woa_ref, wos_ref, gf_ref, wg_ref, wu_ref, wd_ref,
                gl_ref, o_ref):
    x1 = x_ref[0] + _dot(att_ref[0], woa_ref[...]) + _dot(ssm_ref[0], wos_ref[...])
    h = _rms(x1, gf_ref[...]).astype(BF16)
    a = (jax.nn.silu(_dot(h, wg_ref[...])) * _dot(h, wu_ref[...])).astype(BF16)
    o_ref[0] = _rms(x1 + _dot(a, wd_ref[...]), gl_ref[...])


def _ffn(x, att, ssm2, woa, wos, gf, wg, wu, wd, gl, *, tm):
    bsz, seq, d = x.shape
    const = lambda shape: pl.BlockSpec(shape, lambda b, s: (0,) * len(shape),
                                       pipeline_mode=pl.Buffered(1))
    return pl.pallas_call(
        _ffn_kernel,
        grid=(bsz, seq // tm),
        in_specs=[
            pl.BlockSpec((1, tm, d), lambda b, s: (b, s, 0)),
            pl.BlockSpec((1, tm, ATT_WIDTH), lambda b, s: (b, s, 0)),
            pl.BlockSpec((1, tm, SSM_WIDTH), lambda b, s: (b, s, 0)),
            const(woa.shape), const(wos.shape), const(gf.shape),
            const(wg.shape), const(wu.shape), const(wd.shape), const(gl.shape),
        ],
        out_specs=pl.BlockSpec((1, tm, d), lambda b, s: (b, s, 0)),
        out_shape=jax.ShapeDtypeStruct((bsz, seq, d), x.dtype),
        compiler_params=pltpu.CompilerParams(
            dimension_semantics=("arbitrary", "arbitrary"),
            vmem_limit_bytes=V7X_VMEM_LIMIT_BYTES),
        name="outproj_ffn",
    )(x, att, ssm2, woa, wos, gf, wg, wu, wd, gl)


def _t5_bucket(n):
    max_exact = REL_BUCKETS // 2
    nf = jnp.maximum(n, 1).astype(F32)
    large = max_exact + (jnp.log(nf / max_exact) / math.log(REL_MAX_DIST / max_exact)
                         * (REL_BUCKETS - max_exact)).astype(jnp.int32)
    large = jnp.minimum(large, REL_BUCKETS - 1)
    return jnp.where(n < max_exact, n, large)


def _bias_tiles(rel_bias, tq, tk):
    assert tk >= REL_MAX_DIST
    ratio = tq // tk
    nd = tq + tk + 1
    neg = ratio * tk
    table = rel_bias.astype(F32)
    by_dist = table[_t5_bucket(jnp.arange(nd))]
    far = table[_t5_bucket(jnp.full((), REL_MAX_DIST, jnp.int32))]
    by_dist = (by_dist - far[None, :]) * LOG2E
    by_dist = jnp.concatenate([jnp.full((neg, ATT_HEADS), MASK_VALUE, F32), by_dist], axis=0).T
    m = tk + tq
    tiles = []
    for r in range(ratio + 1):
        start = neg - (tk - 1) - (r - 1) * tk
        c = by_dist[:, start:start + m]
        t = jnp.broadcast_to(c[:, None, :], (ATT_HEADS, tk, m)).reshape(ATT_HEADS, tk * m)
        t = t[:, :tk * (m - 1)].reshape(ATT_HEADS, tk, m - 1)
        tiles.append(t[:, :, tk - 1:tk - 1 + tq])
    return jnp.stack(tiles, axis=1)


def _ssm_params(A_re, A_im, log_dt, B_re, B_im, C_re, C_im):
    lam = lax.complex(A_re.astype(F32), A_im.astype(F32))
    dt = jnp.exp(log_dt.astype(F32))[:, None]
    lam_bar = jnp.exp(lam * dt)
    b_bar = ((lam_bar - 1.0) / lam)[:, :, None] * lax.complex(B_re.astype(F32), B_im.astype(F32))
    nlg = SSM_LANE_GROUPS
    gpl = SSM_GROUPS // nlg
    eye = jnp.eye(gpl, dtype=F32)

    def in_blocks(w):
        w = w.reshape(nlg, gpl, SSM_STATE, SSM_GROUP)
        return jnp.einsum('kgnc,gh->kgchn', w, eye).reshape(nlg, gpl * SSM_GROUP, gpl * SSM_STATE)

    def out_blocks(w):
        w = w.reshape(nlg, gpl, SSM_GROUP, SSM_STATE)
        return jnp.einsum('kgcn,gh->kgnhc', w, eye).reshape(nlg, gpl * SSM_STATE, gpl * SSM_GROUP)

    wb = jnp.concatenate([in_blocks(b_bar.real), in_blocks(b_bar.imag)], axis=2).astype(BF16)
    wc = jnp.concatenate([out_blocks(C_re.astype(F32)), out_blocks(-C_im.astype(F32))],
                         axis=1).astype(BF16)
    lam_ri = jnp.concatenate([lam_bar.real.reshape(nlg, 1, -1), lam_bar.imag.reshape(nlg, 1, -1)],
                             axis=2)
    lam_ri = jnp.broadcast_to(lam_ri, (nlg, V7X_SUBLANES, lam_ri.shape[2]))
    return wb, wc, lam_ri


def kernel(x, norm_mix_g, w_in, lambda_q1, lambda_k1, lambda_q2, lambda_k2, subln_g, rel_bias,
           A_re, A_im, log_dt, B_re, B_im, C_re, C_im, D_skip, w_glu, b_glu, ssm_norm_g, w_out,
           norm_ffn_g, w_gate, w_up, w_down, norm_final_g):
    bsz, seq, d = x.shape
    assert d == D_MODEL and bsz == V7X_SUBLANES
    ts, tq, tk, tm = PROJ_ROWS, ATT_TQ, ATT_TK, FFN_ROWS
    assert seq % ts == 0 and seq % tq == 0 and seq % tm == 0 and seq % SSM_STEPS == 0
    assert ts % tk == 0 and tq % tk == 0
    l = 0
    row = lambda v: v.astype(F32).reshape(1, -1)

    w = w_in[l].astype(F32)
    qk_scale = ATT_QK_DIM ** -0.5 * LOG2E
    wqT = (w[:, 0:QK_COLS] * qk_scale).T.astype(BF16)
    wk = w[:, QK_COLS:2 * QK_COLS].astype(BF16)
    wvT = w[:, 2 * QK_COLS:2 * QK_COLS + ATT_WIDTH].T.astype(BF16)
    wu = w[:, 2 * QK_COLS + ATT_WIDTH:].astype(BF16)
    lam_init = 0.8 - 0.6 * math.exp(-0.3 * l)
    lam = (jnp.exp(jnp.sum(lambda_q1[l].astype(F32) * lambda_k1[l].astype(F32)))
           - jnp.exp(jnp.sum(lambda_q2[l].astype(F32) * lambda_k2[l].astype(F32)))
           + lam_init).reshape(1, 1)
    bias = _bias_tiles(rel_bias, tq, tk)
    wb, wc, lam_ri = _ssm_params(A_re[l], A_im[l], log_dt[l], B_re[l], B_im[l], C_re[l], C_im[l])

    qT, k, vT, u = _inproj(x, row(norm_mix_g[l]), wqT, wk, wvT, wu, ts=ts, tk=tk)
    att = _attention(lam, qT, k, vT, bias, subln_g[l].astype(F32).reshape(-1, 1),
                     tq=tq, tk=tk, out_scale=1.0 - lam_init)
    ssm = _ssm(u, wb, wc, lam_ri, row(D_skip[l]),
               w_glu[l].astype(BF16), row(b_glu[l]), row(ssm_norm_g[l]), steps=SSM_STEPS)

    wo = w_out[l].astype(BF16)
    return _ffn(x, att, ssm, wo[0:ATT_WIDTH], wo[ATT_WIDTH:],
                row(norm_ffn_g[l]), w_gate[l].astype(BF16), w_up[l].astype(BF16),
                w_down[l].astype(BF16), row(norm_final_g), tm=tm)
```

```python
import functools
import math

import jax
import jax.numpy as jnp
import numpy as np
from jax import lax
from jax.experimental import pallas as pl
from jax.experimental.pallas import tpu as pltpu

F32 = jnp.float32
BF16 = jnp.bfloat16

D_MODEL = 1024
ATT_WIDTH = 512
SSM_WIDTH = 512
ATT_HEADS = 4
ATT_QK_DIM = 64
ATT_V_DIM = 128
QK_COLS = 512
SSM_GROUP = 16
SSM_GROUPS = 32
SSM_STATE = 64
REL_BUCKETS = 32
REL_MAX_DIST = 128
EPS = 1e-6
LOG2E = math.log2(math.e)
MASK_VALUE = -1e30

V7X_SUBLANES = 8
V7X_LANES = 128
V7X_VMEM_LIMIT_BYTES = 56 * 1024 * 1024

PROJ_ROWS = 512
ATT_TQ = 512
ATT_TK = 512
ATT_COLS = 256
SSM_STEPS = 64
SSM_LANE_GROUPS = 4
FFN_ROWS = 512

_NT = (((1,), (1,)), ((), ()))


def _dot(a, b):
    return jnp.dot(a, b, preferred_element_type=F32)


def _rms(x, g):
    ms = jnp.mean(x * x, axis=-1, keepdims=True)
    return x * lax.rsqrt(ms + EPS) * g


def _inproj_kernel(x_ref, g_ref, wqT_ref, wk_ref, wvT_ref, wu_ref,
                   qT_ref, k_ref, vT_ref, u_ref, *, tk):
    h = _rms(x_ref[0], g_ref[...]).astype(BF16)
    qT = lax.dot_general(wqT_ref[...], h, _NT, preferred_element_type=F32)
    qT_ref[0] = qT.astype(BF16)
    k_ref[0] = _dot(h, wk_ref[...]).astype(BF16)
    vT = lax.dot_general(wvT_ref[...], h, _NT, preferred_element_type=F32).astype(BF16)
    for c in range(vT_ref.shape[1]):
        vT_ref[0, c] = vT[:, c * tk:(c + 1) * tk]
    u_ref[0] = _dot(h, wu_ref[...])


def _inproj(x, g, wqT, wk, wvT, wu, *, ts, tk):
    bsz, seq, d = x.shape
    ns = seq // ts
    const = lambda shape: pl.BlockSpec(shape, lambda b, s: (0,) * len(shape),
                                       pipeline_mode=pl.Buffered(1))
    return pl.pallas_call(
        functools.partial(_inproj_kernel, tk=tk),
        grid=(bsz, ns),
        in_specs=[
            pl.BlockSpec((1, ts, d), lambda b, s: (b, s, 0)),
            const((1, d)),
            const((QK_COLS, d)),
            const((d, QK_COLS)),
            const((ATT_WIDTH, d)),
            const((d, SSM_WIDTH)),
        ],
        out_specs=[
            pl.BlockSpec((1, QK_COLS, ts), lambda b, s: (b, 0, s)),
            pl.BlockSpec((1, ts, QK_COLS), lambda b, s: (b, s, 0)),
            pl.BlockSpec((1, ts // tk, ATT_WIDTH, tk), lambda b, s: (b, s, 0, 0)),
            pl.BlockSpec((1, ts, SSM_WIDTH), lambda b, s: (b, s, 0)),
        ],
        out_shape=[
            jax.ShapeDtypeStruct((bsz, QK_COLS, seq), BF16),
            jax.ShapeDtypeStruct((bsz, seq, QK_COLS), BF16),
            jax.ShapeDtypeStruct((bsz, seq // tk, ATT_WIDTH, tk), BF16),
            jax.ShapeDtypeStruct((bsz, seq, SSM_WIDTH), F32),
        ],
        compiler_params=pltpu.CompilerParams(
            dimension_semantics=("arbitrary", "arbitrary"),
            vmem_limit_bytes=V7X_VMEM_LIMIT_BYTES),
        name="inproj",
    )(x, g, wqT, wk, wvT, wu)


def _attn_kernel(lam_ref, qT_ref, k_ref, vT_ref, bias_ref, g_ref, o_ref,
                 qbd, s_a, s_b, mx_a, mx_b, m_s, l_s, acc_s, *, tq, tk, out_scale):
    qi = pl.program_id(2)
    ratio = tq // tk
    dk = ATT_QK_DIM
    bufs = ((s_a, mx_a), (s_b, mx_b))

    qbd[...] = jnp.zeros_like(qbd)
    for c0 in range(0, tq, ATT_COLS):
        qbd[c0 // ATT_COLS, 0:dk, :] = qT_ref[0, 0:dk, c0:c0 + ATT_COLS]
        qbd[(tq + c0) // ATT_COLS, dk:2 * dk, :] = qT_ref[0, dk:2 * dk, c0:c0 + ATT_COLS]
    m_s[...] = jnp.full_like(m_s, -jnp.inf)
    l_s[...] = jnp.zeros_like(l_s)
    acc_s[...] = jnp.zeros_like(acc_s)

    def live_rows(r, c0):
        if r is None or r == 0:
            return tk
        return max(0, min(tk, c0 % tq + ATT_COLS - (r - 1) * tk))

    def logits(tile, bufs2):
        j, r = tile
        buf, mx = bufs2
        r0 = pl.multiple_of(j * tk, tk)
        for c0 in range(0, 2 * tq, ATT_COLS):
            rows = live_rows(r, c0)
            if rows:
                s = _dot(k_ref[0, pl.ds(r0, rows), :], qbd[c0 // ATT_COLS])
                if r is not None:
                    s = s + bias_ref[0, r, (c0 % tq) // ATT_COLS, 0:rows, :]
                buf[c0 // ATT_COLS, 0:rows, :] = s
                mx[:, c0:c0 + ATT_COLS] = jnp.max(s, axis=0, keepdims=True)

    def softmax_pv(tile, bufs2):
        j, r = tile
        buf, mx = bufs2
        for c0 in range(0, 2 * tq, ATT_COLS):
            rows = live_rows(r, c0)
            if rows == 0:
                continue
            cols = slice(c0, c0 + ATT_COLS)
            m_old = m_s[:, cols]
            m_new = jnp.maximum(m_old, mx[:, cols])
            alpha = jnp.exp2(m_old - m_new)
            p = jnp.exp2(buf[c0 // ATT_COLS, 0:rows, :] - m_new)
            l_s[:, cols] = alpha * l_s[:, cols] + jnp.sum(p, axis=0, keepdims=True)
            cb = c0 // ATT_COLS
            acc_s[cb] = alpha * acc_s[cb] + _dot(vT_ref[0, j, :, 0:rows], p.astype(BF16))
            m_s[:, cols] = m_new

    def run_tiles(tiles, preloaded):
        if not preloaded:
            logits(tiles[0], bufs[0])
        for i, tile in enumerate(tiles):
            if i + 1 < len(tiles):
                logits(tiles[i + 1], bufs[(i + 1) % 2])
            softmax_pv(tile, bufs[i % 2])

    diag = [(qi * ratio + (r - 1), r) for r in range(1, ratio + 1)]

    @pl.when(qi == 0)
    def _():
        run_tiles(diag, preloaded=False)

    @pl.when(qi > 0)
    def _():
        n_pre = qi * ratio
        a, b = bufs

        def pair(p, carry):
            logits((2 * p + 1, None), b)
            softmax_pv((2 * p, None), a)
            logits((2 * p + 2, None), a)
            softmax_pv((2 * p + 1, None), b)
            return carry

        npairs = jnp.maximum(n_pre - 2, 0) // 2
        t = 2 * npairs

        @pl.when(n_pre == 1)
        def _():
            run_tiles([(0, 0)] + diag, preloaded=False)

        @pl.when(n_pre > 1)
        def _():
            logits((0, None), a)
            lax.fori_loop(0, npairs, pair, 0)

            @pl.when(n_pre - t == 2)
            def _():
                run_tiles([(t, None), (t + 1, 0)] + diag, preloaded=True)

            @pl.when(n_pre - t == 3)
            def _():
                run_tiles([(t, None), (t + 1, None), (t + 2, 0)] + diag, preloaded=True)

    lam = lam_ref[0, 0]
    acc = jnp.concatenate([acc_s[cb] for cb in range(2 * tq // ATT_COLS)], axis=1)
    l = l_s[...]
    o = acc[:, 0:tq] / l[:, 0:tq] - lam * (acc[:, tq:2 * tq] / l[:, tq:2 * tq])
    ms = jnp.mean(o * o, axis=0, keepdims=True)
    o = o * lax.rsqrt(ms + EPS) * g_ref[...] * out_scale
    o_ref[0] = o.T.astype(BF16)


def _attention(lam, qT, k, vT, bias, g, *, tq, tk, out_scale):
    bsz, seq, _ = k.shape
    nq = seq // tq
    ratio = tq // tk
    return pl.pallas_call(
        functools.partial(_attn_kernel, tq=tq, tk=tk, out_scale=out_scale),
        grid=(bsz, ATT_HEADS, nq),
        in_specs=[
            pl.BlockSpec(memory_space=pltpu.SMEM),
            pl.BlockSpec((1, 2 * ATT_QK_DIM, tq), lambda b, h, q: (b, h, q)),
            pl.BlockSpec((1, seq, 2 * ATT_QK_DIM), lambda b, h, q: (b, 0, h)),
            pl.BlockSpec((1, seq // tk, ATT_V_DIM, tk), lambda b, h, q: (b, 0, h, 0)),
            pl.BlockSpec((1, ratio + 1, tq // ATT_COLS, tk, ATT_COLS),
                         lambda b, h, q: (h, 0, 0, 0, 0)),
            pl.BlockSpec((ATT_V_DIM, 1), lambda b, h, q: (0, 0)),
        ],
        out_specs=pl.BlockSpec((1, tq, ATT_V_DIM), lambda b, h, q: (b, q, h)),
        out_shape=jax.ShapeDtypeStruct((bsz, seq, ATT_WIDTH), BF16),
        scratch_shapes=[
            pltpu.VMEM((2 * tq // ATT_COLS, 2 * ATT_QK_DIM, ATT_COLS), BF16),
            pltpu.VMEM((2 * tq // ATT_COLS, tk, ATT_COLS), F32),
            pltpu.VMEM((2 * tq // ATT_COLS, tk, ATT_COLS), F32),
            pltpu.VMEM((1, 2 * tq), F32),
            pltpu.VMEM((1, 2 * tq), F32),
            pltpu.VMEM((1, 2 * tq), F32),
            pltpu.VMEM((1, 2 * tq), F32),
            pltpu.VMEM((2 * tq // ATT_COLS, ATT_V_DIM, ATT_COLS), F32),
        ],
        compiler_params=pltpu.CompilerParams(
            dimension_semantics=("arbitrary", "arbitrary", "arbitrary"),
            vmem_limit_bytes=V7X_VMEM_LIMIT_BYTES),
        name="diffattn",
    )(lam, qT, k, vT, bias, g)


def _ssm_kernel(u_ref, wb_ref, wc_ref, lam_ref, d_ref, wglu_ref, bglu_ref, g_ref,
                o_ref, uin, res, xs, st, *, steps):
    rows = V7X_SUBLANES
    half = xs.shape[2] // 2
    lanes = V7X_LANES
    bsz = u_ref.shape[0]

    @pl.when(pl.program_id(0) == 0)
    def _():
        st[...] = jnp.zeros_like(st)

    for b in range(bsz):
        for c in range(SSM_LANE_GROUPS):
            uin[c, pl.ds(b, steps, stride=bsz), :] = u_ref[b, :, c * lanes:(c + 1) * lanes]

    ys = []
    for c in range(SSM_LANE_GROUPS):
        xs[c] = _dot(uin[c].astype(BF16), wb_ref[c])
        lre = lam_ref[c, :, 0:half]
        lim = lam_ref[c, :, half:2 * half]

        def step(t, carry, c=c, lre=lre, lim=lim):
            sre, sim = carry
            r = pl.multiple_of(t * rows, rows)
            bre = xs[c, pl.ds(r, rows), 0:half]
            bim = xs[c, pl.ds(r, rows), half:2 * half]
            nre = lre * sre - lim * sim + bre
            nim = lre * sim + lim * sre + bim
            xs[c, pl.ds(r, rows), 0:half] = nre
            xs[c, pl.ds(r, rows), half:2 * half] = nim
            return nre, nim

        sre, sim = lax.fori_loop(0, steps, step,
                                 (st[c, :, 0:half], st[c, :, half:2 * half]), unroll=True)
        st[c, :, 0:half] = sre
        st[c, :, half:2 * half] = sim
        ys.append(_dot(xs[c].astype(BF16), wc_ref[c]))

    u = jnp.concatenate([uin[c] for c in range(SSM_LANE_GROUPS)], axis=1)
    y = jnp.concatenate(ys, axis=1) + d_ref[...] * u
    gl = jax.nn.gelu(y)
    z = _dot(gl.astype(BF16), wglu_ref[...]) + bglu_ref[...]
    out = _rms(gl * jax.nn.sigmoid(z), g_ref[...])
    for c in range(SSM_LANE_GROUPS):
        res[c] = out[:, c * lanes:(c + 1) * lanes]
    for b in range(bsz):
        for c in range(SSM_LANE_GROUPS):
            o_ref[b, :, c * lanes:(c + 1) * lanes] = (
                res[c, pl.ds(b, steps, stride=bsz), :].astype(BF16))


def _ssm(u, wb, wc, lam, dskip, wglu, bglu, g, *, steps):
    bsz, seq, width = u.shape
    r = steps * bsz
    const = lambda shape: pl.BlockSpec(shape, lambda i: (0,) * len(shape),
                                       pipeline_mode=pl.Buffered(1))
    return pl.pallas_call(
        functools.partial(_ssm_kernel, steps=steps),
        grid=(seq // steps,),
        in_specs=[
            pl.BlockSpec((bsz, steps, width), lambda i: (0, i, 0)),
            const(wb.shape), const(wc.shape), const(lam.shape), const(dskip.shape),
            const(wglu.shape), const(bglu.shape), const(g.shape),
        ],
        out_specs=pl.BlockSpec((bsz, steps, width), lambda i: (0, i, 0)),
        out_shape=jax.ShapeDtypeStruct((bsz, seq, width), BF16),
        scratch_shapes=[
            pltpu.VMEM((SSM_LANE_GROUPS, r, V7X_LANES), F32),
            pltpu.VMEM((SSM_LANE_GROUPS, r, V7X_LANES), F32),
            pltpu.VMEM((SSM_LANE_GROUPS, r, wb.shape[2]), F32),
            pltpu.VMEM((SSM_LANE_GROUPS, V7X_SUBLANES, wb.shape[2]), F32),
        ],
        compiler_params=pltpu.CompilerParams(
            dimension_semantics=("arbitrary",),
            vmem_limit_bytes=V7X_VMEM_LIMIT_BYTES),
        name="s5ssm",
    )(u, wb, wc, lam, dskip, wglu, bglu, g)


def _ffn_kernel(x_ref, att_ref, ssm_ref, woa_ref, wos_ref, gf_ref, wg_ref, wu_ref, wd_ref,
                gl_ref, o_ref):
    x1 = x_ref[0] + _dot(att_ref[0], woa_ref[...]) + _dot(ssm_ref[0], wos_ref[...])
    h = _rms(x1, gf_ref[...]).astype(BF16)
    a = (jax.nn.silu(_dot(h, wg_ref[...])) * _dot(h, wu_ref[...])).astype(BF16)
    o_ref[0] = _rms(x1 + _dot(a, wd_ref[...]), gl_ref[...])


def _ffn(x, att, ssm2, woa, wos, gf, wg, wu, wd, gl, *, tm):
    bsz, seq, d = x.shape
    const = lambda shape: pl.BlockSpec(shape, lambda b, s: (0,) * len(shape),
                                       pipeline_mode=pl.Buffered(1))
    return pl.pallas_call(
        _ffn_kernel,
        grid=(bsz, seq // tm),
        in_specs=[
            pl.BlockSpec((1, tm, d), lambda b, s: (b, s, 0)),
            pl.BlockSpec((1, tm, ATT_WIDTH), lambda b, s: (b, s, 0)),
            pl.BlockSpec((1, tm, SSM_WIDTH), lambda b, s: (b, s, 0)),
            const(woa.shape), const(wos.shape), const(gf.shape),
            const(wg.shape), const(wu.shape), const(wd.shape), const(gl.shape),
        ],
        out_specs=pl.BlockSpec((1, tm, d), lambda b, s: (b, s, 0)),
        out_shape=jax.ShapeDtypeStruct((bsz, seq, d), x.dtype),
        compiler_params=pltpu.CompilerParams(
            dimension_semantics=("arbitrary", "arbitrary"),
            vmem_limit_bytes=V7X_VMEM_LIMIT_BYTES),
        name="outproj_ffn",
    )(x, att, ssm2, woa, wos, gf, wg, wu, wd, gl)


def _t5_bucket(n):
    max_exact = REL_BUCKETS // 2
    nf = jnp.maximum(n, 1).astype(F32)
    large = max_exact + (jnp.log(nf / max_exact) / math.log(REL_MAX_DIST / max_exact)
                         * (REL_BUCKETS - max_exact)).astype(jnp.int32)
    large = jnp.minimum(large, REL_BUCKETS - 1)
    return jnp.where(n < max_exact, n, large)


def _bias_tiles(rel_bias, tq, tk):
    assert tk >= REL_MAX_DIST
    ratio = tq // tk
    nd = tq + tk + 1
    neg = ratio * tk
    table = rel_bias.astype(F32)
    by_dist = table[_t5_bucket(jnp.arange(nd))]
    far = table[_t5_bucket(jnp.full((), REL_MAX_DIST, jnp.int32))]
    by_dist = (by_dist - far[None, :]) * LOG2E
    by_dist = jnp.concatenate([jnp.full((neg, ATT_HEADS), MASK_VALUE, F32), by_dist], axis=0).T
    m = tk + tq
    tiles = []
    for r in range(ratio + 1):
        start = neg - (tk - 1) - (r - 1) * tk
        c = by_dist[:, start:start + m]
        t = jnp.broadcast_to(c[:, None, :], (ATT_HEADS, tk, m)).reshape(ATT_HEADS, tk * m)
        t = t[:, :tk * (m - 1)].reshape(ATT_HEADS, tk, m - 1)
        tiles.append(t[:, :, tk - 1:tk - 1 + tq])
    tiles = jnp.stack(tiles, axis=1)
    tiles = tiles.reshape(ATT_HEADS, ratio + 1, tk, tq // ATT_COLS, ATT_COLS)
    return jnp.transpose(tiles, (0, 1, 3, 2, 4))


def _ssm_params(A_re, A_im, log_dt, B_re, B_im, C_re, C_im):
    lam = lax.complex(A_re.astype(F32), A_im.astype(F32))
    dt = jnp.exp(log_dt.astype(F32))[:, None]
    lam_bar = jnp.exp(lam * dt)
    b_bar = ((lam_bar - 1.0) / lam)[:, :, None] * lax.complex(B_re.astype(F32), B_im.astype(F32))
    nlg = SSM_LANE_GROUPS
    gpl = SSM_GROUPS // nlg
    eye = jnp.eye(gpl, dtype=F32)

    def in_blocks(w):
        w = w.reshape(nlg, gpl, SSM_STATE, SSM_GROUP)
        return jnp.einsum('kgnc,gh->kgchn', w, eye).reshape(nlg, gpl * SSM_GROUP, gpl * SSM_STATE)

    def out_blocks(w):
        w = w.reshape(nlg, gpl, SSM_GROUP, SSM_STATE)
        return jnp.einsum('kgcn,gh->kgnhc', w, eye).reshape(nlg, gpl * SSM_STATE, gpl * SSM_GROUP)

    wb = jnp.concatenate([in_blocks(b_bar.real), in_blocks(b_bar.imag)], axis=2).astype(BF16)
    wc = jnp.concatenate([out_blocks(C_re.astype(F32)), out_blocks(-C_im.astype(F32))],
                         axis=1).astype(BF16)
    lam_ri = jnp.concatenate([lam_bar.real.reshape(nlg, 1, -1), lam_bar.imag.reshape(nlg, 1, -1)],
                             axis=2)
    lam_ri = jnp.broadcast_to(lam_ri, (nlg, V7X_SUBLANES, lam_ri.shape[2]))
    return wb, wc, lam_ri


def kernel(x, norm_mix_g, w_in, lambda_q1, lambda_k1, lambda_q2, lambda_k2, subln_g, rel_bias,
           A_re, A_im, log_dt, B_re, B_im, C_re, C_im, D_skip, w_glu, b_glu, ssm_norm_g, w_out,
           norm_ffn_g, w_gate, w_up, w_down, norm_final_g):
    bsz, seq, d = x.shape
    assert d == D_MODEL and bsz == V7X_SUBLANES
    ts, tq, tk, tm = PROJ_ROWS, ATT_TQ, ATT_TK, FFN_ROWS
    assert seq % ts == 0 and seq % tq == 0 and seq % tm == 0 and seq % SSM_STEPS == 0
    assert ts % tk == 0 and tq % tk == 0
    l = 0
    row = lambda v: v.astype(F32).reshape(1, -1)

    w = w_in[l].astype(F32)
    qk_scale = ATT_QK_DIM ** -0.5 * LOG2E
    wqT = (w[:, 0:QK_COLS] * qk_scale).T.astype(BF16)
    wk = w[:, QK_COLS:2 * QK_COLS].astype(BF16)
    wvT = w[:, 2 * QK_COLS:2 * QK_COLS + ATT_WIDTH].T.astype(BF16)
    wu = w[:, 2 * QK_COLS + ATT_WIDTH:].astype(BF16)
    lam_init = 0.8 - 0.6 * math.exp(-0.3 * l)
    lam = (jnp.exp(jnp.sum(lambda_q1[l].astype(F32) * lambda_k1[l].astype(F32)))
           - jnp.exp(jnp.sum(lambda_q2[l].astype(F32) * lambda_k2[l].astype(F32)))
           + lam_init).reshape(1, 1)
    bias = _bias_tiles(rel_bias, tq, tk)
    wb, wc, lam_ri = _ssm_params(A_re[l], A_im[l], log_dt[l], B_re[l], B_im[l], C_re[l], C_im[l])

    qT, k, vT, u = _inproj(x, row(norm_mix_g[l]), wqT, wk, wvT, wu, ts=ts, tk=tk)
    att = _attention(lam, qT, k, vT, bias, subln_g[l].astype(F32).reshape(-1, 1),
                     tq=tq, tk=tk, out_scale=1.0 - lam_init)
    ssm = _ssm(u, wb, wc, lam_ri, row(D_skip[l]),
               w_glu[l].astype(BF16), row(b_glu[l]), row(ssm_norm_g[l]), steps=SSM_STEPS)

    wo = w_out[l].astype(BF16)
    return _ffn(x, att, ssm, wo[0:ATT_WIDTH], wo[ATT_WIDTH:],
                row(norm_ffn_g[l]), w_gate[l].astype(BF16), w_up[l].astype(BF16),
                w_down[l].astype(BF16), row(norm_final_g), tm=tm)
```

```python
import functools
import math

import jax
import jax.numpy as jnp
import numpy as np
from jax import lax
from jax.experimental import pallas as pl
from jax.experimental.pallas import tpu as pltpu

F32 = jnp.float32
BF16 = jnp.bfloat16

D_MODEL = 1024
ATT_WIDTH = 512
SSM_WIDTH = 512
ATT_HEADS = 4
ATT_QK_DIM = 64
ATT_V_DIM = 128
ATT_V_ROWS = ATT_V_DIM + 16
QK_COLS = 512
SSM_GROUP = 16
SSM_GROUPS = 32
SSM_STATE = 64
REL_BUCKETS = 32
REL_MAX_DIST = 128
EPS = 1e-6
LOG2E = math.log2(math.e)
MASK_VALUE = -1e30

V7X_SUBLANES = 8
V7X_LANES = 128
V7X_VMEM_LIMIT_BYTES = 56 * 1024 * 1024

PROJ_ROWS = 512
ATT_TQ = 512
ATT_TK = 512
ATT_COLS = 256
SSM_STEPS = 64
SSM_LANE_GROUPS = 4
FFN_ROWS = 512

_NT = (((1,), (1,)), ((), ()))


def _dot(a, b):
    return jnp.dot(a, b, preferred_element_type=F32)


def _rms(x, g):
    ms = jnp.mean(x * x, axis=-1, keepdims=True)
    return x * lax.rsqrt(ms + EPS) * g


def _inproj_kernel(x_ref, g_ref, wqT_ref, wk_ref, wvT_ref, wu_ref,
                   qT_ref, k_ref, vT_ref, u_ref, *, tk):
    h = _rms(x_ref[0], g_ref[...]).astype(BF16)
    qT = lax.dot_general(wqT_ref[...], h, _NT, preferred_element_type=F32)
    qT_ref[0] = qT.astype(BF16)
    k_ref[0] = _dot(h, wk_ref[...]).astype(BF16)
    vT = lax.dot_general(wvT_ref[...], h, _NT, preferred_element_type=F32).astype(BF16)
    ones = jnp.ones((ATT_V_ROWS - ATT_V_DIM, tk), BF16)
    for c in range(vT_ref.shape[1]):
        for hd in range(ATT_HEADS):
            r0 = hd * ATT_V_ROWS
            vT_ref[0, c, r0:r0 + ATT_V_DIM, :] = vT[hd * ATT_V_DIM:(hd + 1) * ATT_V_DIM,
                                                    c * tk:(c + 1) * tk]
            vT_ref[0, c, r0 + ATT_V_DIM:r0 + ATT_V_ROWS, :] = ones
    u_ref[0] = _dot(h, wu_ref[...])


def _inproj(x, g, wqT, wk, wvT, wu, *, ts, tk):
    bsz, seq, d = x.shape
    ns = seq // ts
    const = lambda shape: pl.BlockSpec(shape, lambda b, s: (0,) * len(shape),
                                       pipeline_mode=pl.Buffered(1))
    return pl.pallas_call(
        functools.partial(_inproj_kernel, tk=tk),
        grid=(bsz, ns),
        in_specs=[
            pl.BlockSpec((1, ts, d), lambda b, s: (b, s, 0)),
            const((1, d)),
            const((QK_COLS, d)),
            const((d, QK_COLS)),
            const((ATT_WIDTH, d)),
            const((d, SSM_WIDTH)),
        ],
        out_specs=[
            pl.BlockSpec((1, QK_COLS, ts), lambda b, s: (b, 0, s)),
            pl.BlockSpec((1, ts, QK_COLS), lambda b, s: (b, s, 0)),
            pl.BlockSpec((1, ts // tk, ATT_HEADS * ATT_V_ROWS, tk), lambda b, s: (b, s, 0, 0)),
            pl.BlockSpec((1, ts, SSM_WIDTH), lambda b, s: (b, s, 0)),
        ],
        out_shape=[
            jax.ShapeDtypeStruct((bsz, QK_COLS, seq), BF16),
            jax.ShapeDtypeStruct((bsz, seq, QK_COLS), BF16),
            jax.ShapeDtypeStruct((bsz, seq // tk, ATT_HEADS * ATT_V_ROWS, tk), BF16),
            jax.ShapeDtypeStruct((bsz, seq, SSM_WIDTH), F32),
        ],
        compiler_params=pltpu.CompilerParams(
            dimension_semantics=("arbitrary", "arbitrary"),
            vmem_limit_bytes=V7X_VMEM_LIMIT_BYTES),
        name="inproj",
    )(x, g, wqT, wk, wvT, wu)


def _attn_kernel(lam_ref, qT_ref, k_ref, vT_ref, bias_ref, g_ref, o_ref,
                 qbd, s_a, s_b, mx_a, mx_b, m_s, acc_s, *, tq, tk, out_scale):
    qi = pl.program_id(2)
    ratio = tq // tk
    dk = ATT_QK_DIM
    bufs = ((s_a, mx_a), (s_b, mx_b))

    qbd[...] = jnp.zeros_like(qbd)
    for c0 in range(0, tq, ATT_COLS):
        qbd[c0 // ATT_COLS, 0:dk, :] = qT_ref[0, 0:dk, c0:c0 + ATT_COLS]
        qbd[(tq + c0) // ATT_COLS, dk:2 * dk, :] = qT_ref[0, dk:2 * dk, c0:c0 + ATT_COLS]
    m_s[...] = jnp.full_like(m_s, -jnp.inf)
    acc_s[...] = jnp.zeros_like(acc_s)

    def live_rows(r, c0):
        if r is None or r == 0:
            return tk
        return max(0, min(tk, c0 % tq + ATT_COLS - (r - 1) * tk))

    def logits(tile, bufs2):
        j, r = tile
        buf, mx = bufs2
        r0 = pl.multiple_of(j * tk, tk)
        for c0 in range(0, 2 * tq, ATT_COLS):
            rows = live_rows(r, c0)
            if rows:
                s = _dot(k_ref[0, pl.ds(r0, rows), :], qbd[c0 // ATT_COLS])
                if r is not None:
                    s = s + bias_ref[0, r, (c0 % tq) // ATT_COLS, 0:rows, :]
                buf[c0 // ATT_COLS, 0:rows, :] = s
                mx[:, c0:c0 + ATT_COLS] = jnp.max(s, axis=0, keepdims=True)

    def softmax_pv(tile, bufs2):
        j, r = tile
        buf, mx = bufs2
        for c0 in range(0, 2 * tq, ATT_COLS):
            rows = live_rows(r, c0)
            if rows == 0:
                continue
            cols = slice(c0, c0 + ATT_COLS)
            m_old = m_s[:, cols]
            m_new = jnp.maximum(m_old, mx[:, cols])
            alpha = jnp.exp2(m_old - m_new)
            p = jnp.exp2(buf[c0 // ATT_COLS, 0:rows, :] - m_new)
            cb = c0 // ATT_COLS
            acc_s[cb] = alpha * acc_s[cb] + _dot(vT_ref[0, j, :, 0:rows], p.astype(BF16))
            m_s[:, cols] = m_new

    def run_tiles(tiles, preloaded):
        if not preloaded:
            logits(tiles[0], bufs[0])
        for i, tile in enumerate(tiles):
            if i + 1 < len(tiles):
                logits(tiles[i + 1], bufs[(i + 1) % 2])
            softmax_pv(tile, bufs[i % 2])

    diag = [(qi * ratio + (r - 1), r) for r in range(1, ratio + 1)]

    @pl.when(qi == 0)
    def _():
        run_tiles(diag, preloaded=False)

    @pl.when(qi > 0)
    def _():
        n_pre = qi * ratio
        a, b = bufs

        def pair(p, carry):
            logits((2 * p + 1, None), b)
            softmax_pv((2 * p, None), a)
            logits((2 * p + 2, None), a)
            softmax_pv((2 * p + 1, None), b)
            return carry

        npairs = jnp.maximum(n_pre - 2, 0) // 2
        t = 2 * npairs

        @pl.when(n_pre == 1)
        def _():
            run_tiles([(0, 0)] + diag, preloaded=False)

        @pl.when(n_pre > 1)
        def _():
            logits((0, None), a)
            lax.fori_loop(0, npairs, pair, 0)

            @pl.when(n_pre - t == 2)
            def _():
                run_tiles([(t, None), (t + 1, 0)] + diag, preloaded=True)

            @pl.when(n_pre - t == 3)
            def _():
                run_tiles([(t, None), (t + 1, None), (t + 2, 0)] + diag, preloaded=True)

    lam = lam_ref[0, 0]
    acc = jnp.concatenate([acc_s[cb, 0:ATT_V_DIM, :] for cb in range(2 * tq // ATT_COLS)], axis=1)
    l = jnp.concatenate([acc_s[cb, ATT_V_DIM:ATT_V_DIM + 1, :]
                         for cb in range(2 * tq // ATT_COLS)], axis=1)
    o = acc[:, 0:tq] / l[:, 0:tq] - lam * (acc[:, tq:2 * tq] / l[:, tq:2 * tq])
    ms = jnp.mean(o * o, axis=0, keepdims=True)
    o = o * lax.rsqrt(ms + EPS) * g_ref[...] * out_scale
    o_ref[0] = o.T.astype(BF16)


def _attention(lam, qT, k, vT, bias, g, *, tq, tk, out_scale):
    bsz, seq, _ = k.shape
    nq = seq // tq
    ratio = tq // tk
    return pl.pallas_call(
        functools.partial(_attn_kernel, tq=tq, tk=tk, out_scale=out_scale),
        grid=(bsz, ATT_HEADS, nq),
        in_specs=[
            pl.BlockSpec(memory_space=pltpu.SMEM),
            pl.BlockSpec((1, 2 * ATT_QK_DIM, tq), lambda b, h, q: (b, h, q)),
            pl.BlockSpec((1, seq, 2 * ATT_QK_DIM), lambda b, h, q: (b, 0, h)),
            pl.BlockSpec((1, seq // tk, ATT_V_ROWS, tk), lambda b, h, q: (b, 0, h, 0)),
            pl.BlockSpec((1, ratio + 1, tq // ATT_COLS, tk, ATT_COLS),
                         lambda b, h, q: (h, 0, 0, 0, 0)),
            pl.BlockSpec((ATT_V_DIM, 1), lambda b, h, q: (0, 0)),
        ],
        out_specs=pl.BlockSpec((1, tq, ATT_V_DIM), lambda b, h, q: (b, q, h)),
        out_shape=jax.ShapeDtypeStruct((bsz, seq, ATT_WIDTH), BF16),
        scratch_shapes=[
            pltpu.VMEM((2 * tq // ATT_COLS, 2 * ATT_QK_DIM, ATT_COLS), BF16),
            pltpu.VMEM((2 * tq // ATT_COLS, tk, ATT_COLS), F32),
            pltpu.VMEM((2 * tq // ATT_COLS, tk, ATT_COLS), F32),
            pltpu.VMEM((1, 2 * tq), F32),
            pltpu.VMEM((1, 2 * tq), F32),
            pltpu.VMEM((1, 2 * tq), F32),
            pltpu.VMEM((2 * tq // ATT_COLS, ATT_V_ROWS, ATT_COLS), F32),
        ],
        compiler_params=pltpu.CompilerParams(
            dimension_semantics=("arbitrary", "arbitrary", "arbitrary"),
            vmem_limit_bytes=V7X_VMEM_LIMIT_BYTES),
        name="diffattn",
    )(lam, qT, k, vT, bias, g)


def _ssm_kernel(u_ref, wb_ref, wc_ref, lam_ref, d_ref, wglu_ref, bglu_ref, g_ref,
                o_ref, uin, res, xs, st, *, steps):
    rows = V7X_SUBLANES
    half = xs.shape[2] // 2
    lanes = V7X_LANES
    bsz = u_ref.shape[0]

    @pl.when(pl.program_id(0) == 0)
    def _():
        st[...] = jnp.zeros_like(st)

    for b in range(bsz):
        for c in range(SSM_LANE_GROUPS):
            uin[c, pl.ds(b, steps, stride=bsz), :] = u_ref[b, :, c * lanes:(c + 1) * lanes]

    ys = []
    for c in range(SSM_LANE_GROUPS):
        xs[c] = _dot(uin[c].astype(BF16), wb_ref[c])
        lre = lam_ref[c, :, 0:half]
        lim = lam_ref[c, :, half:2 * half]

        def step(t, carry, c=c, lre=lre, lim=lim):
            sre, sim = carry
            r = pl.multiple_of(t * rows, rows)
            bre = xs[c, pl.ds(r, rows), 0:half]
            bim = xs[c, pl.ds(r, rows), half:2 * half]
            nre = lre * sre - lim * sim + bre
            nim = lre * sim + lim * sre + bim
            xs[c, pl.ds(r, rows), 0:half] = nre
            xs[c, pl.ds(r, rows), half:2 * half] = nim
            return nre, nim

        sre, sim = lax.fori_loop(0, steps, step,
                                 (st[c, :, 0:half], st[c, :, half:2 * half]), unroll=True)
        st[c, :, 0:half] = sre
        st[c, :, half:2 * half] = sim
        ys.append(_dot(xs[c].astype(BF16), wc_ref[c]))

    u = jnp.concatenate([uin[c] for c in range(SSM_LANE_GROUPS)], axis=1)
    y = jnp.concatenate(ys, axis=1) + d_ref[...] * u
    gl = jax.nn.gelu(y)
    z = _dot(gl.astype(BF16), wglu_ref[...]) + bglu_ref[...]
    out = _rms(gl * jax.nn.sigmoid(z), g_ref[...])
    for c in range(SSM_LANE_GROUPS):
        res[c] = out[:, c * lanes:(c + 1) * lanes]
    for b in range(bsz):
        for c in range(SSM_LANE_GROUPS):
            o_ref[b, :, c * lanes:(c + 1) * lanes] = (
                res[c, pl.ds(b, steps, stride=bsz), :].astype(BF16))


def _ssm(u, wb, wc, lam, dskip, wglu, bglu, g, *, steps):
    bsz, seq, width = u.shape
    r = steps * bsz
    const = lambda shape: pl.BlockSpec(shape, lambda i: (0,) * len(shape),
                                       pipeline_mode=pl.Buffered(1))
    return pl.pallas_call(
        functools.partial(_ssm_kernel, steps=steps),
        grid=(seq // steps,),
        in_specs=[
            pl.BlockSpec((bsz, steps, width), lambda i: (0, i, 0)),
            const(wb.shape), const(wc.shape), const(lam.shape), const(dskip.shape),
            const(wglu.shape), const(bglu.shape), const(g.shape),
        ],
        out_specs=pl.BlockSpec((bsz, steps, width), lambda i: (0, i, 0)),
        out_shape=jax.ShapeDtypeStruct((bsz, seq, width), BF16),
        scratch_shapes=[
            pltpu.VMEM((SSM_LANE_GROUPS, r, V7X_LANES), F32),
            pltpu.VMEM((SSM_LANE_GROUPS, r, V7X_LANES), F32),
            pltpu.VMEM((SSM_LANE_GROUPS, r, wb.shape[2]), F32),
            pltpu.VMEM((SSM_LANE_GROUPS, V7X_SUBLANES, wb.shape[2]), F32),
        ],
        compiler_params=pltpu.CompilerParams(
            dimension_semantics=("arbitrary",),
            vmem_limit_bytes=V7X_VMEM_LIMIT_BYTES),
        name="s5ssm",
    )(u, wb, wc, lam, dskip, wglu, bglu, g)


def _ffn_kernel(x_ref, att_ref, ssm_ref, woa_ref, wos_ref, gf_ref, wg_ref, wu_ref, wd_ref,
                gl_ref, o_ref):
    x1 = x_ref[0] + _dot(att_ref[0], woa_ref[...]) + _dot(ssm_ref[0], wos_ref[...])
    h = _rms(x1, gf_ref[...]).astype(BF16)
    a = (jax.nn.silu(_dot(h, wg_ref[...])) * _dot(h, wu_ref[...])).astype(BF16)
    o_ref[0] = _rms(x1 + _dot(a, wd_ref[...]), gl_ref[...])


def _ffn(x, att, ssm2, woa, wos, gf, wg, wu, wd, gl, *, tm):
    bsz, seq, d = x.shape
    const = lambda shape: pl.BlockSpec(shape, lambda b, s: (0,) * len(shape),
                                       pipeline_mode=pl.Buffered(1))
    return pl.pallas_call(
        _ffn_kernel,
        grid=(bsz, seq // tm),
        in_specs=[
            pl.BlockSpec((1, tm, d), lambda b, s: (b, s, 0)),
            pl.BlockSpec((1, tm, ATT_WIDTH), lambda b, s: (b, s, 0)),
            pl.BlockSpec((1, tm, SSM_WIDTH), lambda b, s: (b, s, 0)),
            const(woa.shape), const(wos.shape), const(gf.shape),
            const(wg.shape), const(wu.shape), const(wd.shape), const(gl.shape),
        ],
        out_specs=pl.BlockSpec((1, tm, d), lambda b, s: (b, s, 0)),
        out_shape=jax.ShapeDtypeStruct((bsz, seq, d), x.dtype),
        compiler_params=pltpu.CompilerParams(
            dimension_semantics=("arbitrary", "arbitrary"),
            vmem_limit_bytes=V7X_VMEM_LIMIT_BYTES),
        name="outproj_ffn",
    )(x, att, ssm2, woa, wos, gf, wg, wu, wd, gl)


def _t5_bucket(n):
    max_exact = REL_BUCKETS // 2
    nf = jnp.maximum(n, 1).astype(F32)
    large = max_exact + (jnp.log(nf / max_exact) / math.log(REL_MAX_DIST / max_exact)
                         * (REL_BUCKETS - max_exact)).astype(jnp.int32)
    large = jnp.minimum(large, REL_BUCKETS - 1)
    return jnp.where(n < max_exact, n, large)


def _bias_tiles(rel_bias, tq, tk):
    assert tk >= REL_MAX_DIST
    ratio = tq // tk
    nd = tq + tk + 1
    neg = ratio * tk
    table = rel_bias.astype(F32)
    by_dist = table[_t5_bucket(jnp.arange(nd))]
    far = table[_t5_bucket(jnp.full((), REL_MAX_DIST, jnp.int32))]
    by_dist = (by_dist - far[None, :]) * LOG2E
    by_dist = jnp.concatenate([jnp.full((neg, ATT_HEADS), MASK_VALUE, F32), by_dist], axis=0).T
    m = tk + tq
    tiles = []
    for r in range(ratio + 1):
        start = neg - (tk - 1) - (r - 1) * tk
        c = by_dist[:, start:start + m]
        t = jnp.broadcast_to(c[:, None, :], (ATT_HEADS, tk, m)).reshape(ATT_HEADS, tk * m)
        t = t[:, :tk * (m - 1)].reshape(ATT_HEADS, tk, m - 1)
        tiles.append(t[:, :, tk - 1:tk - 1 + tq])
    tiles = jnp.stack(tiles, axis=1)
    tiles = tiles.reshape(ATT_HEADS, ratio + 1, tk, tq // ATT_COLS, ATT_COLS)
    return jnp.transpose(tiles, (0, 1, 3, 2, 4))


def _ssm_params(A_re, A_im, log_dt, B_re, B_im, C_re, C_im):
    lam = lax.complex(A_re.astype(F32), A_im.astype(F32))
    dt = jnp.exp(log_dt.astype(F32))[:, None]
    lam_bar = jnp.exp(lam * dt)
    b_bar = ((lam_bar - 1.0) / lam)[:, :, None] * lax.complex(B_re.astype(F32), B_im.astype(F32))
    nlg = SSM_LANE_GROUPS
    gpl = SSM_GROUPS // nlg
    eye = jnp.eye(gpl, dtype=F32)

    def in_blocks(w):
        w = w.reshape(nlg, gpl, SSM_STATE, SSM_GROUP)
        return jnp.einsum('kgnc,gh->kgchn', w, eye).reshape(nlg, gpl * SSM_GROUP, gpl * SSM_STATE)

    def out_blocks(w):
        w = w.reshape(nlg, gpl, SSM_GROUP, SSM_STATE)
        return jnp.einsum('kgcn,gh->kgnhc', w, eye).reshape(nlg, gpl * SSM_STATE, gpl * SSM_GROUP)

    wb = jnp.concatenate([in_blocks(b_bar.real), in_blocks(b_bar.imag)], axis=2).astype(BF16)
    wc = jnp.concatenate([out_blocks(C_re.astype(F32)), out_blocks(-C_im.astype(F32))],
                         axis=1).astype(BF16)
    lam_ri = jnp.concatenate([lam_bar.real.reshape(nlg, 1, -1), lam_bar.imag.reshape(nlg, 1, -1)],
                             axis=2)
    lam_ri = jnp.broadcast_to(lam_ri, (nlg, V7X_SUBLANES, lam_ri.shape[2]))
    return wb, wc, lam_ri


def kernel(x, norm_mix_g, w_in, lambda_q1, lambda_k1, lambda_q2, lambda_k2, subln_g, rel_bias,
           A_re, A_im, log_dt, B_re, B_im, C_re, C_im, D_skip, w_glu, b_glu, ssm_norm_g, w_out,
           norm_ffn_g, w_gate, w_up, w_down, norm_final_g):
    bsz, seq, d = x.shape
    assert d == D_MODEL and bsz == V7X_SUBLANES
    ts, tq, tk, tm = PROJ_ROWS, ATT_TQ, ATT_TK, FFN_ROWS
    assert seq % ts == 0 and seq % tq == 0 and seq % tm == 0 and seq % SSM_STEPS == 0
    assert ts % tk == 0 and tq % tk == 0
    l = 0
    row = lambda v: v.astype(F32).reshape(1, -1)

    w = w_in[l].astype(F32)
    qk_scale = ATT_QK_DIM ** -0.5 * LOG2E
    wqT = (w[:, 0:QK_COLS] * qk_scale).T.astype(BF16)
    wk = w[:, QK_COLS:2 * QK_COLS].astype(BF16)
    wvT = w[:, 2 * QK_COLS:2 * QK_COLS + ATT_WIDTH].T.astype(BF16)
    wu = w[:, 2 * QK_COLS + ATT_WIDTH:].astype(BF16)
    lam_init = 0.8 - 0.6 * math.exp(-0.3 * l)
    lam = (jnp.exp(jnp.sum(lambda_q1[l].astype(F32) * lambda_k1[l].astype(F32)))
           - jnp.exp(jnp.sum(lambda_q2[l].astype(F32) * lambda_k2[l].astype(F32)))
           + lam_init).reshape(1, 1)
    bias = _bias_tiles(rel_bias, tq, tk)
    wb, wc, lam_ri = _ssm_params(A_re[l], A_im[l], log_dt[l], B_re[l], B_im[l], C_re[l], C_im[l])

    qT, k, vT, u = _inproj(x, row(norm_mix_g[l]), wqT, wk, wvT, wu, ts=ts, tk=tk)
    att = _attention(lam, qT, k, vT, bias, subln_g[l].astype(F32).reshape(-1, 1),
                     tq=tq, tk=tk, out_scale=1.0 - lam_init)
    ssm = _ssm(u, wb, wc, lam_ri, row(D_skip[l]),
               w_glu[l].astype(BF16), row(b_glu[l]), row(ssm_norm_g[l]), steps=SSM_STEPS)

    wo = w_out[l].astype(BF16)
    return _ffn(x, att, ssm, wo[0:ATT_WIDTH], wo[ATT_WIDTH:],
                row(norm_ffn_g[l]), w_gate[l].astype(BF16), w_up[l].astype(BF16),
                w_down[l].astype(BF16), row(norm_final_g), tm=tm)
```

```python
import functools
import math

import jax
import jax.numpy as jnp
from jax import lax
from jax.experimental import pallas as pl
from jax.experimental.pallas import tpu as pltpu

F32 = jnp.float32
BF16 = jnp.bfloat16

D_MODEL = 1024
ATT_WIDTH = 512
SSM_WIDTH = 512
ATT_HEADS = 4
ATT_QK_DIM = 64
ATT_V_DIM = 128
ATT_V_ROWS = ATT_V_DIM + 16
QK_COLS = 512
SSM_GROUP = 16
SSM_GROUPS = 32
SSM_STATE = 64
REL_BUCKETS = 32
REL_MAX_DIST = 128
EPS = 1e-6
LOG2E = math.log2(math.e)
MASK_VALUE = -1e30

V7X_SUBLANES = 8
V7X_LANES = 128
V7X_VMEM_LIMIT_BYTES = 56 * 1024 * 1024

PROJ_ROWS = 1024
ATT_TQ = 512
ATT_TK = 512
ATT_COLS = 256
SSM_STEPS = 64
SSM_LANE_GROUPS = 4
FFN_ROWS = 512

_NT = (((1,), (1,)), ((), ()))


def _dot(a, b):
    return jnp.dot(a, b, preferred_element_type=F32)


def _rms(x, g):
    ms = jnp.mean(x * x, axis=-1, keepdims=True)
    return x * lax.rsqrt(ms + EPS) * g


def _inproj_kernel(x_ref, g_ref, wqT_ref, wk_ref, wvT_ref, wu_ref,
                   qT_ref, k_ref, vT_ref, u_ref, *, tq, tk):
    h = _rms(x_ref[0], g_ref[...]).astype(BF16)
    qT = lax.dot_general(wqT_ref[...], h, _NT, preferred_element_type=F32)
    qT = qT.astype(BF16)
    for c in range(qT_ref.shape[1]):
        for hd in range(ATT_HEADS):
            qT_ref[0, c, hd] = qT[hd * 2 * ATT_QK_DIM:(hd + 1) * 2 * ATT_QK_DIM,
                                  c * tq:(c + 1) * tq]
    k_ref[0] = _dot(h, wk_ref[...]).astype(BF16)
    vT = lax.dot_general(wvT_ref[...], h, _NT, preferred_element_type=F32).astype(BF16)
    ones = jnp.ones((ATT_V_ROWS - ATT_V_DIM, tk), BF16)
    for c in range(vT_ref.shape[1]):
        for hd in range(ATT_HEADS):
            r0 = hd * ATT_V_ROWS
            vT_ref[0, c, r0:r0 + ATT_V_DIM, :] = vT[hd * ATT_V_DIM:(hd + 1) * ATT_V_DIM,
                                                    c * tk:(c + 1) * tk]
            vT_ref[0, c, r0 + ATT_V_DIM:r0 + ATT_V_ROWS, :] = ones
    u_ref[0] = _dot(h, wu_ref[...])


def _inproj(x, g, wqT, wk, wvT, wu, *, ts, tq, tk):
    bsz, seq, d = x.shape
    ns = seq // ts
    const = lambda shape: pl.BlockSpec(shape, lambda b, s: (0,) * len(shape),
                                       pipeline_mode=pl.Buffered(1))
    return pl.pallas_call(
        functools.partial(_inproj_kernel, tq=tq, tk=tk),
        grid=(bsz, ns),
        in_specs=[
            pl.BlockSpec((1, ts, d), lambda b, s: (b, s, 0)),
            const((1, d)),
            const((QK_COLS, d)),
            const((d, QK_COLS)),
            const((ATT_WIDTH, d)),
            const((d, SSM_WIDTH)),
        ],
        out_specs=[
            pl.BlockSpec((1, ts // tq, ATT_HEADS, 2 * ATT_QK_DIM, tq),
                         lambda b, s: (b, s, 0, 0, 0)),
            pl.BlockSpec((1, ts, QK_COLS), lambda b, s: (b, s, 0)),
            pl.BlockSpec((1, ts // tk, ATT_HEADS * ATT_V_ROWS, tk), lambda b, s: (b, s, 0, 0)),
            pl.BlockSpec((1, ts, SSM_WIDTH), lambda b, s: (b, s, 0)),
        ],
        out_shape=[
            jax.ShapeDtypeStruct((bsz, seq // tq, ATT_HEADS, 2 * ATT_QK_DIM, tq), BF16),
            jax.ShapeDtypeStruct((bsz, seq, QK_COLS), BF16),
            jax.ShapeDtypeStruct((bsz, seq // tk, ATT_HEADS * ATT_V_ROWS, tk), BF16),
            jax.ShapeDtypeStruct((bsz, seq, SSM_WIDTH), F32),
        ],
        compiler_params=pltpu.CompilerParams(
            dimension_semantics=("arbitrary", "arbitrary"),
            vmem_limit_bytes=V7X_VMEM_LIMIT_BYTES),
        name="inproj",
    )(x, g, wqT, wk, wvT, wu)


def _attn_kernel(lam_ref, qT_ref, k_ref, vT_ref, bias_ref, g_ref, o_ref,
                 qbd, s_a, s_b, mx_a, mx_b, m_s, acc_s, *, tq, tk, out_scale):
    assert tq == tk
    nq = qT_ref.shape[1]
    dk = ATT_QK_DIM
    a, b = bufs = ((s_a, mx_a), (s_b, mx_b))
    nblk = 2 * tq // ATT_COLS

    def q_init(qi):
        qbd[...] = jnp.zeros_like(qbd)
        for c0 in range(0, tq, ATT_COLS):
            qbd[c0 // ATT_COLS, 0:dk, :] = qT_ref[0, qi, 0, 0:dk, c0:c0 + ATT_COLS]
            qbd[(tq + c0) // ATT_COLS, dk:2 * dk, :] = qT_ref[0, qi, 0, dk:2 * dk,
                                                              c0:c0 + ATT_COLS]
        m_s[...] = jnp.full_like(m_s, -jnp.inf)
        acc_s[...] = jnp.zeros_like(acc_s)

    def finalize(qi):
        lam = lam_ref[0, 0]
        acc = jnp.concatenate([acc_s[cb, 0:ATT_V_DIM, :] for cb in range(nblk)], axis=1)
        l = jnp.concatenate([acc_s[cb, ATT_V_DIM:ATT_V_DIM + 1, :]
                             for cb in range(nblk)], axis=1)
        o = acc[:, 0:tq] / l[:, 0:tq] - lam * (acc[:, tq:2 * tq] / l[:, tq:2 * tq])
        ms = jnp.mean(o * o, axis=0, keepdims=True)
        o = o * lax.rsqrt(ms + EPS) * g_ref[...] * out_scale
        o_ref[0, pl.ds(pl.multiple_of(qi * tq, tq), tq), :] = o.T.astype(BF16)

    def live_rows(r, c0):
        if r is None or r == 0:
            return tk
        return max(0, min(tk, c0 % tq + ATT_COLS - (r - 1) * tk))

    def logits(tile, bufs2):
        j, r = tile
        buf, mx = bufs2
        r0 = pl.multiple_of(j * tk, tk)
        for c0 in range(0, 2 * tq, ATT_COLS):
            rows = live_rows(r, c0)
            if rows:
                s = _dot(k_ref[0, pl.ds(r0, rows), :], qbd[c0 // ATT_COLS])
                if r is not None:
                    s = s + bias_ref[0, r, (c0 % tq) // ATT_COLS, 0:rows, :]
                buf[c0 // ATT_COLS, 0:rows, :] = s
                mx[:, c0:c0 + ATT_COLS] = jnp.max(s, axis=0, keepdims=True)

    def softmax_pv(tile, bufs2):
        j, r = tile
        buf, mx = bufs2
        for c0 in range(0, 2 * tq, ATT_COLS):
            rows = live_rows(r, c0)
            if rows == 0:
                continue
            cols = slice(c0, c0 + ATT_COLS)
            m_old = m_s[:, cols]
            m_new = jnp.maximum(m_old, mx[:, cols])
            alpha = jnp.exp2(m_old - m_new)
            p = jnp.exp2(buf[c0 // ATT_COLS, 0:rows, :] - m_new)
            cb = c0 // ATT_COLS
            acc_s[cb] = alpha * acc_s[cb] + _dot(vT_ref[0, j, :, 0:rows], p.astype(BF16))
            m_s[:, cols] = m_new

    def run_tiles(tiles, preloaded):
        if not preloaded:
            logits(tiles[0], bufs[0])
        for i, tile in enumerate(tiles):
            if i + 1 < len(tiles):
                logits(tiles[i + 1], bufs[(i + 1) % 2])
            softmax_pv(tile, bufs[i % 2])

    q_init(0)
    run_tiles([(0, 1)], preloaded=False)
    finalize(0)
    q_init(1)
    run_tiles([(0, 0), (1, 1)], preloaded=False)

    def q_block(qi, carry):
        finalize(qi - 1)
        q_init(qi)
        logits((0, None), a)

        def pair(p, c):
            logits((2 * p + 1, None), b)
            softmax_pv((2 * p, None), a)
            logits((2 * p + 2, None), a)
            softmax_pv((2 * p + 1, None), b)
            return c

        npairs = (qi - 2) // 2
        lax.fori_loop(0, npairs, pair, 0)
        t = 2 * npairs

        @pl.when(qi - t == 2)
        def _():
            run_tiles([(t, None), (t + 1, 0), (qi, 1)], preloaded=True)

        @pl.when(qi - t == 3)
        def _():
            run_tiles([(t, None), (t + 1, None), (t + 2, 0), (qi, 1)], preloaded=True)

        return carry

    lax.fori_loop(2, nq, q_block, 0)
    finalize(nq - 1)


def _attention(lam, qT, k, vT, bias, g, *, tq, tk, out_scale):
    bsz, seq, _ = k.shape
    nq = seq // tq
    assert tq == tk and nq >= 2
    return pl.pallas_call(
        functools.partial(_attn_kernel, tq=tq, tk=tk, out_scale=out_scale),
        grid=(bsz, ATT_HEADS),
        in_specs=[
            pl.BlockSpec(memory_space=pltpu.SMEM),
            pl.BlockSpec((1, nq, 1, 2 * ATT_QK_DIM, tq), lambda b, h: (b, 0, h, 0, 0)),
            pl.BlockSpec((1, seq, 2 * ATT_QK_DIM), lambda b, h: (b, 0, h)),
            pl.BlockSpec((1, seq // tk, ATT_V_ROWS, tk), lambda b, h: (b, 0, h, 0)),
            pl.BlockSpec((1, 2, tq // ATT_COLS, tk, ATT_COLS), lambda b, h: (h, 0, 0, 0, 0)),
            pl.BlockSpec((ATT_V_DIM, 1), lambda b, h: (0, 0)),
        ],
        out_specs=pl.BlockSpec((1, seq, ATT_V_DIM), lambda b, h: (b, 0, h)),
        out_shape=jax.ShapeDtypeStruct((bsz, seq, ATT_WIDTH), BF16),
        scratch_shapes=[
            pltpu.VMEM((2 * tq // ATT_COLS, 2 * ATT_QK_DIM, ATT_COLS), BF16),
            pltpu.VMEM((2 * tq // ATT_COLS, tk, ATT_COLS), F32),
            pltpu.VMEM((2 * tq // ATT_COLS, tk, ATT_COLS), F32),
            pltpu.VMEM((1, 2 * tq), F32),
            pltpu.VMEM((1, 2 * tq), F32),
            pltpu.VMEM((1, 2 * tq), F32),
            pltpu.VMEM((2 * tq // ATT_COLS, ATT_V_ROWS, ATT_COLS), F32),
        ],
        compiler_params=pltpu.CompilerParams(
            dimension_semantics=("arbitrary", "arbitrary"),
            vmem_limit_bytes=V7X_VMEM_LIMIT_BYTES),
        name="diffattn",
    )(lam, qT, k, vT, bias, g)


def _ssm_kernel(u_ref, wb_ref, wc_ref, lam_ref, d_ref, wglu_ref, bglu_ref, g_ref,
                o_ref, uin, res, xs, st, *, steps):
    rows = V7X_SUBLANES
    half = xs.shape[2] // 2
    lanes = V7X_LANES
    bsz = u_ref.shape[0]

    @pl.when(pl.program_id(0) == 0)
    def _():
        st[...] = jnp.zeros_like(st)

    for b in range(bsz):
        for c in range(SSM_LANE_GROUPS):
            uin[c, pl.ds(b, steps, stride=bsz), :] = u_ref[b, :, c * lanes:(c + 1) * lanes]

    ys = []
    for c in range(SSM_LANE_GROUPS):
        xs[c] = _dot(uin[c].astype(BF16), wb_ref[c])
        lre = lam_ref[c, :, 0:half]
        lim = lam_ref[c, :, half:2 * half]

        def step(t, carry, c=c, lre=lre, lim=lim):
            sre, sim = carry
            r = pl.multiple_of(t * rows, rows)
            bre = xs[c, pl.ds(r, rows), 0:half]
            bim = xs[c, pl.ds(r, rows), half:2 * half]
            nre = lre * sre - lim * sim + bre
            nim = lre * sim + lim * sre + bim
            xs[c, pl.ds(r, rows), 0:half] = nre
            xs[c, pl.ds(r, rows), half:2 * half] = nim
            return nre, nim

        sre, sim = lax.fori_loop(0, steps, step,
                                 (st[c, :, 0:half], st[c, :, half:2 * half]), unroll=True)
        st[c, :, 0:half] = sre
        st[c, :, half:2 * half] = sim
        ys.append(_dot(xs[c].astype(BF16), wc_ref[c]))

    u = jnp.concatenate([uin[c] for c in range(SSM_LANE_GROUPS)], axis=1)
    y = jnp.concatenate(ys, axis=1) + d_ref[...] * u
    gl = jax.nn.gelu(y)
    z = _dot(gl.astype(BF16), wglu_ref[...]) + bglu_ref[...]
    out = _rms(gl * jax.nn.sigmoid(z), g_ref[...])
    for c in range(SSM_LANE_GROUPS):
        res[c] = out[:, c * lanes:(c + 1) * lanes]
    for b in range(bsz):
        for c in range(SSM_LANE_GROUPS):
            o_ref[b, :, c * lanes:(c + 1) * lanes] = (
                res[c, pl.ds(b, steps, stride=bsz), :].astype(BF16))


def _ssm(u, wb, wc, lam, dskip, wglu, bglu, g, *, steps):
    bsz, seq, width = u.shape
    r = steps * bsz
    const = lambda shape: pl.BlockSpec(shape, lambda i: (0,) * len(shape),
                                       pipeline_mode=pl.Buffered(1))
    return pl.pallas_call(
        functools.partial(_ssm_kernel, steps=steps),
        grid=(seq // steps,),
        in_specs=[
            pl.BlockSpec((bsz, steps, width), lambda i: (0, i, 0)),
            const(wb.shape), const(wc.shape), const(lam.shape), const(dskip.shape),
            const(wglu.shape), const(bglu.shape), const(g.shape),
        ],
        out_specs=pl.BlockSpec((bsz, steps, width), lambda i: (0, i, 0)),
        out_shape=jax.ShapeDtypeStruct((bsz, seq, width), BF16),
        scratch_shapes=[
            pltpu.VMEM((SSM_LANE_GROUPS, r, V7X_LANES), F32),
            pltpu.VMEM((SSM_LANE_GROUPS, r, V7X_LANES), F32),
            pltpu.VMEM((SSM_LANE_GROUPS, r, wb.shape[2]), F32),
            pltpu.VMEM((SSM_LANE_GROUPS, V7X_SUBLANES, wb.shape[2]), F32),
        ],
        compiler_params=pltpu.CompilerParams(
            dimension_semantics=("arbitrary",),
            vmem_limit_bytes=V7X_VMEM_LIMIT_BYTES),
        name="s5ssm",
    )(u, wb, wc, lam, dskip, wglu, bglu, g)


def _ffn_kernel(x_ref, att_ref, ssm_ref, woa_ref, wos_ref, gf_ref, wg_ref, wu_ref, wd_ref,
                gl_ref, o_ref):
    x1 = x_ref[0] + _dot(att_ref[0], woa_ref[...]) + _dot(ssm_ref[0], wos_ref[...])
    h = _rms(x1, gf_ref[...]).astype(BF16)
    a = (jax.nn.silu(_dot(h, wg_ref[...])) * _dot(h, wu_ref[...])).astype(BF16)
    o_ref[0] = _rms(x1 + _dot(a, wd_ref[...]), gl_ref[...])


def _ffn(x, att, ssm, woa, wos, gf, wg, wu, wd, gl, *, tm):
    bsz, seq, d = x.shape
    const = lambda shape: pl.BlockSpec(shape, lambda b, s: (0,) * len(shape),
                                       pipeline_mode=pl.Buffered(1))
    return pl.pallas_call(
        _ffn_kernel,
        grid=(bsz, seq // tm),
        in_specs=[
            pl.BlockSpec((1, tm, d), lambda b, s: (b, s, 0)),
            pl.BlockSpec((1, tm, ATT_WIDTH), lambda b, s: (b, s, 0)),
            pl.BlockSpec((1, tm, SSM_WIDTH), lambda b, s: (b, s, 0)),
            const(woa.shape), const(wos.shape), const(gf.shape),
            const(wg.shape), const(wu.shape), const(wd.shape), const(gl.shape),
        ],
        out_specs=pl.BlockSpec((1, tm, d), lambda b, s: (b, s, 0)),
        out_shape=jax.ShapeDtypeStruct((bsz, seq, d), x.dtype),
        compiler_params=pltpu.CompilerParams(
            dimension_semantics=("arbitrary", "arbitrary"),
            vmem_limit_bytes=V7X_VMEM_LIMIT_BYTES),
        name="outproj_ffn",
    )(x, att, ssm, woa, wos, gf, wg, wu, wd, gl)


def _t5_bucket(n):
    max_exact = REL_BUCKETS // 2
    nf = jnp.maximum(n, 1).astype(F32)
    large = max_exact + (jnp.log(nf / max_exact) / math.log(REL_MAX_DIST / max_exact)
                         * (REL_BUCKETS - max_exact)).astype(jnp.int32)
    large = jnp.minimum(large, REL_BUCKETS - 1)
    return jnp.where(n < max_exact, n, large)


def _bias_tiles(rel_bias, tq, tk):
    assert tk >= REL_MAX_DIST
    ratio = tq // tk
    nd = tq + tk + 1
    neg = ratio * tk
    table = rel_bias.astype(F32)
    by_dist = table[_t5_bucket(jnp.arange(nd))]
    far = table[_t5_bucket(jnp.full((), REL_MAX_DIST, jnp.int32))]
    by_dist = (by_dist - far[None, :]) * LOG2E
    by_dist = jnp.concatenate([jnp.full((neg, ATT_HEADS), MASK_VALUE, F32), by_dist], axis=0).T
    m = tk + tq
    tiles = []
    for r in range(ratio + 1):
        start = neg - (tk - 1) - (r - 1) * tk
        c = by_dist[:, start:start + m]
        t = jnp.broadcast_to(c[:, None, :], (ATT_HEADS, tk, m)).reshape(ATT_HEADS, tk * m)
        t = t[:, :tk * (m - 1)].reshape(ATT_HEADS, tk, m - 1)
        tiles.append(t[:, :, tk - 1:tk - 1 + tq])
    tiles = jnp.stack(tiles, axis=1)
    tiles = tiles.reshape(ATT_HEADS, ratio + 1, tk, tq // ATT_COLS, ATT_COLS)
    return jnp.transpose(tiles, (0, 1, 3, 2, 4))


def _ssm_params(A_re, A_im, log_dt, B_re, B_im, C_re, C_im):
    lam = lax.complex(A_re.astype(F32), A_im.astype(F32))
    dt = jnp.exp(log_dt.astype(F32))[:, None]
    lam_bar = jnp.exp(lam * dt)
    b_bar = ((lam_bar - 1.0) / lam)[:, :, None] * lax.complex(B_re.astype(F32), B_im.astype(F32))
    nlg = SSM_LANE_GROUPS
    gpl = SSM_GROUPS // nlg
    eye = jnp.eye(gpl, dtype=F32)

    def in_blocks(w):
        w = w.reshape(nlg, gpl, SSM_STATE, SSM_GROUP)
        return jnp.einsum('kgnc,gh->kgchn', w, eye).reshape(nlg, gpl * SSM_GROUP, gpl * SSM_STATE)

    def out_blocks(w):
        w = w.reshape(nlg, gpl, SSM_GROUP, SSM_STATE)
        return jnp.einsum('kgcn,gh->kgnhc', w, eye).reshape(nlg, gpl * SSM_STATE, gpl * SSM_GROUP)

    wb = jnp.concatenate([in_blocks(b_bar.real), in_blocks(b_bar.imag)], axis=2).astype(BF16)
    wc = jnp.concatenate([out_blocks(C_re.astype(F32)), out_blocks(-C_im.astype(F32))],
                         axis=1).astype(BF16)
    lam_ri = jnp.concatenate([lam_bar.real.reshape(nlg, 1, -1), lam_bar.imag.reshape(nlg, 1, -1)],
                             axis=2)
    lam_ri = jnp.broadcast_to(lam_ri, (nlg, V7X_SUBLANES, lam_ri.shape[2]))
    return wb, wc, lam_ri


def kernel(x, norm_mix_g, w_in, lambda_q1, lambda_k1, lambda_q2, lambda_k2, subln_g, rel_bias,
           A_re, A_im, log_dt, B_re, B_im, C_re, C_im, D_skip, w_glu, b_glu, ssm_norm_g, w_out,
           norm_ffn_g, w_gate, w_up, w_down, norm_final_g):
    bsz, seq, d = x.shape
    assert d == D_MODEL and bsz == V7X_SUBLANES
    ts, tq, tk, tm = PROJ_ROWS, ATT_TQ, ATT_TK, FFN_ROWS
    assert seq % ts == 0 and seq % tq == 0 and seq % tm == 0 and seq % SSM_STEPS == 0
    assert ts % tk == 0 and ts % tq == 0 and tq == tk
    l = 0
    row = lambda v: v.astype(F32).reshape(1, -1)

    w = w_in[l].astype(F32)
    qk_scale = ATT_QK_DIM ** -0.5 * LOG2E
    wqT = (w[:, 0:QK_COLS] * qk_scale).T.astype(BF16)
    wk = w[:, QK_COLS:2 * QK_COLS].astype(BF16)
    wvT = w[:, 2 * QK_COLS:2 * QK_COLS + ATT_WIDTH].T.astype(BF16)
    wu = w[:, 2 * QK_COLS + ATT_WIDTH:].astype(BF16)
    lam_init = 0.8 - 0.6 * math.exp(-0.3 * l)
    lam = (jnp.exp(jnp.sum(lambda_q1[l].astype(F32) * lambda_k1[l].astype(F32)))
           - jnp.exp(jnp.sum(lambda_q2[l].astype(F32) * lambda_k2[l].astype(F32)))
           + lam_init).reshape(1, 1)
    bias = _bias_tiles(rel_bias, tq, tk)
    wb, wc, lam_ri = _ssm_params(A_re[l], A_im[l], log_dt[l], B_re[l], B_im[l], C_re[l], C_im[l])

    qT, k, vT, u = _inproj(x, row(norm_mix_g[l]), wqT, wk, wvT, wu, ts=ts, tq=tq, tk=tk)
    att = _attention(lam, qT, k, vT, bias, subln_g[l].astype(F32).reshape(-1, 1),
                     tq=tq, tk=tk, out_scale=1.0 - lam_init)
    ssm = _ssm(u, wb, wc, lam_ri, row(D_skip[l]),
               w_glu[l].astype(BF16), row(b_glu[l]), row(ssm_norm_g[l]), steps=SSM_STEPS)

    wo = w_out[l].astype(BF16)
    return _ffn(x, att, ssm, wo[0:ATT_WIDTH], wo[ATT_WIDTH:],
                row(norm_ffn_g[l]), w_gate[l].astype(BF16), w_up[l].astype(BF16),
                w_down[l].astype(BF16), row(norm_final_g), tm=tm)
```

```python
import functools
import math

import jax
import jax.numpy as jnp
from jax import lax
from jax.experimental import pallas as pl
from jax.experimental.pallas import tpu as pltpu

F32 = jnp.float32
BF16 = jnp.bfloat16

D_MODEL = 1024
ATT_WIDTH = 512
SSM_WIDTH = 512
ATT_HEADS = 4
ATT_QK_DIM = 64
ATT_V_DIM = 128
ATT_V_ROWS = ATT_V_DIM + 16
QK_COLS = 512
SSM_GROUP = 16
SSM_GROUPS = 32
SSM_STATE = 64
REL_BUCKETS = 32
REL_MAX_DIST = 128
EPS = 1e-6
LOG2E = math.log2(math.e)
MASK_VALUE = -1e30

V7X_SUBLANES = 8
V7X_LANES = 128
V7X_VMEM_LIMIT_BYTES = 56 * 1024 * 1024

PROJ_ROWS = 1024
ATT_TQ = 512
ATT_TK = 512
ATT_COLS = 256
SSM_STEPS = 64
SSM_LANE_GROUPS = 4
FFN_ROWS = 512

_NT = (((1,), (1,)), ((), ()))


def _dot(a, b):
    return jnp.dot(a, b, preferred_element_type=F32)


def _rms(x, g):
    ms = jnp.mean(x * x, axis=-1, keepdims=True)
    return x * lax.rsqrt(ms + EPS) * g


def _inproj_kernel(x_ref, g_ref, wqT_ref, wk_ref, wvT_ref, wu_ref,
                   qT_ref, k_ref, vT_ref, u_ref, *, tq, tk):
    h = _rms(x_ref[0], g_ref[...]).astype(BF16)
    qT = lax.dot_general(wqT_ref[...], h, _NT, preferred_element_type=F32)
    qT = qT.astype(BF16)
    for c in range(qT_ref.shape[1]):
        for hd in range(ATT_HEADS):
            qT_ref[0, c, hd] = qT[hd * 2 * ATT_QK_DIM:(hd + 1) * 2 * ATT_QK_DIM,
                                  c * tq:(c + 1) * tq]
    k_ref[0] = _dot(h, wk_ref[...]).astype(BF16)
    vT = lax.dot_general(wvT_ref[...], h, _NT, preferred_element_type=F32).astype(BF16)
    ones = jnp.ones((ATT_V_ROWS - ATT_V_DIM, tk), BF16)
    for c in range(vT_ref.shape[1]):
        for hd in range(ATT_HEADS):
            r0 = hd * ATT_V_ROWS
            vT_ref[0, c, r0:r0 + ATT_V_DIM, :] = vT[hd * ATT_V_DIM:(hd + 1) * ATT_V_DIM,
                                                    c * tk:(c + 1) * tk]
            vT_ref[0, c, r0 + ATT_V_DIM:r0 + ATT_V_ROWS, :] = ones
    u_ref[0] = _dot(h, wu_ref[...])


def _inproj(x, g, wqT, wk, wvT, wu, *, ts, tq, tk):
    bsz, seq, d = x.shape
    ns = seq // ts
    const = lambda shape: pl.BlockSpec(shape, lambda b, s: (0,) * len(shape),
                                       pipeline_mode=pl.Buffered(1))
    return pl.pallas_call(
        functools.partial(_inproj_kernel, tq=tq, tk=tk),
        grid=(bsz, ns),
        in_specs=[
            pl.BlockSpec((1, ts, d), lambda b, s: (b, s, 0)),
            const((1, d)),
            const((QK_COLS, d)),
            const((d, QK_COLS)),
            const((ATT_WIDTH, d)),
            const((d, SSM_WIDTH)),
        ],
        out_specs=[
            pl.BlockSpec((1, ts // tq, ATT_HEADS, 2 * ATT_QK_DIM, tq),
                         lambda b, s: (b, s, 0, 0, 0)),
            pl.BlockSpec((1, ts, QK_COLS), lambda b, s: (b, s, 0)),
            pl.BlockSpec((1, ts // tk, ATT_HEADS * ATT_V_ROWS, tk), lambda b, s: (b, s, 0, 0)),
            pl.BlockSpec((1, ts, SSM_WIDTH), lambda b, s: (b, s, 0)),
        ],
        out_shape=[
            jax.ShapeDtypeStruct((bsz, seq // tq, ATT_HEADS, 2 * ATT_QK_DIM, tq), BF16),
            jax.ShapeDtypeStruct((bsz, seq, QK_COLS), BF16),
            jax.ShapeDtypeStruct((bsz, seq // tk, ATT_HEADS * ATT_V_ROWS, tk), BF16),
            jax.ShapeDtypeStruct((bsz, seq, SSM_WIDTH), F32),
        ],
        compiler_params=pltpu.CompilerParams(
            dimension_semantics=("arbitrary", "arbitrary"),
            vmem_limit_bytes=V7X_VMEM_LIMIT_BYTES),
        name="inproj",
    )(x, g, wqT, wk, wvT, wu)


def _attn_kernel(lam_ref, qT_ref, k_ref, vT_ref, bias_ref, g_ref, o_ref,
                 qbd, s_a, s_b, mx_a, mx_b, m_s, acc_s, *, tq, tk, out_scale):
    assert tq == tk
    nq = qT_ref.shape[1]
    dk = ATT_QK_DIM
    a, b = bufs = ((s_a, mx_a), (s_b, mx_b))
    nblk = 2 * tq // ATT_COLS

    def q_init(qi):
        qbd[...] = jnp.zeros_like(qbd)
        for c0 in range(0, tq, ATT_COLS):
            qbd[c0 // ATT_COLS, 0:dk, :] = qT_ref[0, qi, 0, 0:dk, c0:c0 + ATT_COLS]
            qbd[(tq + c0) // ATT_COLS, dk:2 * dk, :] = qT_ref[0, qi, 0, dk:2 * dk,
                                                              c0:c0 + ATT_COLS]
        m_s[...] = jnp.full_like(m_s, -jnp.inf)
        acc_s[...] = jnp.zeros_like(acc_s)

    def finalize(qi):
        lam = lam_ref[0, 0]
        acc = jnp.concatenate([acc_s[cb, 0:ATT_V_DIM, :] for cb in range(nblk)], axis=1)
        l = jnp.concatenate([acc_s[cb, ATT_V_DIM:ATT_V_DIM + 1, :]
                             for cb in range(nblk)], axis=1)
        o = acc[:, 0:tq] / l[:, 0:tq] - lam * (acc[:, tq:2 * tq] / l[:, tq:2 * tq])
        ms = jnp.mean(o * o, axis=0, keepdims=True)
        o = o * lax.rsqrt(ms + EPS) * g_ref[...] * out_scale
        o_ref[0, pl.ds(pl.multiple_of(qi * tq, tq), tq), :] = o.T.astype(BF16)

    def live_rows(r, c0):
        if r is None or r == 0:
            return tk
        return max(0, min(tk, c0 % tq + ATT_COLS - (r - 1) * tk))

    def logits_block(tile, bufs2, c0):
        j, r = tile
        buf, mx = bufs2
        rows = live_rows(r, c0)
        if rows:
            r0 = pl.multiple_of(j * tk, tk)
            s = _dot(k_ref[0, pl.ds(r0, rows), :], qbd[c0 // ATT_COLS])
            if r is not None:
                s = s + bias_ref[0, r, (c0 % tq) // ATT_COLS, 0:rows, :]
            buf[c0 // ATT_COLS, 0:rows, :] = s
            mx[:, c0:c0 + ATT_COLS] = jnp.max(s, axis=0, keepdims=True)

    def softmax_pv_block(tile, bufs2, c0):
        j, r = tile
        buf, mx = bufs2
        rows = live_rows(r, c0)
        if rows:
            cols = slice(c0, c0 + ATT_COLS)
            cb = c0 // ATT_COLS
            m_old = m_s[:, cols]
            m_new = jnp.maximum(m_old, mx[:, cols])
            alpha = jnp.exp2(m_old - m_new)
            p = jnp.exp2(buf[cb, 0:rows, :] - m_new)
            acc_s[cb] = alpha * acc_s[cb] + _dot(vT_ref[0, j, :, 0:rows], p.astype(BF16))
            m_s[:, cols] = m_new

    def stage(nxt, nbuf, cur, cbuf):
        for c0 in range(0, 2 * tq, ATT_COLS):
            if nxt is not None:
                logits_block(nxt, nbuf, c0)
            if cur is not None:
                softmax_pv_block(cur, cbuf, c0)

    def logits(tile, bufs2):
        stage(tile, bufs2, None, None)

    def run_tiles(tiles, preloaded):
        if not preloaded:
            logits(tiles[0], bufs[0])
        for i, tile in enumerate(tiles):
            nxt = tiles[i + 1] if i + 1 < len(tiles) else None
            stage(nxt, bufs[(i + 1) % 2], tile, bufs[i % 2])

    q_init(0)
    run_tiles([(0, 1)], preloaded=False)
    finalize(0)
    q_init(1)
    run_tiles([(0, 0), (1, 1)], preloaded=False)

    def q_block(qi, carry):
        finalize(qi - 1)
        q_init(qi)
        logits((0, None), a)

        def pair(p, c):
            stage((2 * p + 1, None), b, (2 * p, None), a)
            stage((2 * p + 2, None), a, (2 * p + 1, None), b)
            return c

        npairs = (qi - 2) // 2
        lax.fori_loop(0, npairs, pair, 0)
        t = 2 * npairs

        @pl.when(qi - t == 2)
        def _():
            run_tiles([(t, None), (t + 1, 0), (qi, 1)], preloaded=True)

        @pl.when(qi - t == 3)
        def _():
            run_tiles([(t, None), (t + 1, None), (t + 2, 0), (qi, 1)], preloaded=True)

        return carry

    lax.fori_loop(2, nq, q_block, 0)
    finalize(nq - 1)


def _attention(lam, qT, k, vT, bias, g, *, tq, tk, out_scale):
    bsz, seq, _ = k.shape
    nq = seq // tq
    assert tq == tk and nq >= 2
    return pl.pallas_call(
        functools.partial(_attn_kernel, tq=tq, tk=tk, out_scale=out_scale),
        grid=(bsz, ATT_HEADS),
        in_specs=[
            pl.BlockSpec(memory_space=pltpu.SMEM),
            pl.BlockSpec((1, nq, 1, 2 * ATT_QK_DIM, tq), lambda b, h: (b, 0, h, 0, 0)),
            pl.BlockSpec((1, seq, 2 * ATT_QK_DIM), lambda b, h: (b, 0, h)),
            pl.BlockSpec((1, seq // tk, ATT_V_ROWS, tk), lambda b, h: (b, 0, h, 0)),
            pl.BlockSpec((1, 2, tq // ATT_COLS, tk, ATT_COLS), lambda b, h: (h, 0, 0, 0, 0)),
            pl.BlockSpec((ATT_V_DIM, 1), lambda b, h: (0, 0)),
        ],
        out_specs=pl.BlockSpec((1, seq, ATT_V_DIM), lambda b, h: (b, 0, h)),
        out_shape=jax.ShapeDtypeStruct((bsz, seq, ATT_WIDTH), BF16),
        scratch_shapes=[
            pltpu.VMEM((2 * tq // ATT_COLS, 2 * ATT_QK_DIM, ATT_COLS), BF16),
            pltpu.VMEM((2 * tq // ATT_COLS, tk, ATT_COLS), F32),
            pltpu.VMEM((2 * tq // ATT_COLS, tk, ATT_COLS), F32),
            pltpu.VMEM((1, 2 * tq), F32),
            pltpu.VMEM((1, 2 * tq), F32),
            pltpu.VMEM((1, 2 * tq), F32),
            pltpu.VMEM((2 * tq // ATT_COLS, ATT_V_ROWS, ATT_COLS), F32),
        ],
        compiler_params=pltpu.CompilerParams(
            dimension_semantics=("arbitrary", "arbitrary"),
            vmem_limit_bytes=V7X_VMEM_LIMIT_BYTES),
        name="diffattn",
    )(lam, qT, k, vT, bias, g)


def _ssm_kernel(u_ref, wb_ref, wc_ref, lam_ref, d_ref, wglu_ref, bglu_ref, g_ref,
                o_ref, uin, res, xs, st, *, steps):
    rows = V7X_SUBLANES
    half = xs.shape[2] // 2
    lanes = V7X_LANES
    bsz = u_ref.shape[0]

    @pl.when(pl.program_id(0) == 0)
    def _():
        st[...] = jnp.zeros_like(st)

    for b in range(bsz):
        for c in range(SSM_LANE_GROUPS):
            uin[c, pl.ds(b, steps, stride=bsz), :] = u_ref[b, :, c * lanes:(c + 1) * lanes]

    ys = []
    for c in range(SSM_LANE_GROUPS):
        xs[c] = _dot(uin[c].astype(BF16), wb_ref[c])
        lre = lam_ref[c, :, 0:half]
        lim = lam_ref[c, :, half:2 * half]

        def step(t, carry, c=c, lre=lre, lim=lim):
            sre, sim = carry
            r = pl.multiple_of(t * rows, rows)
            bre = xs[c, pl.ds(r, rows), 0:half]
            bim = xs[c, pl.ds(r, rows), half:2 * half]
            nre = lre * sre - lim * sim + bre
            nim = lre * sim + lim * sre + bim
            xs[c, pl.ds(r, rows), 0:half] = nre
            xs[c, pl.ds(r, rows), half:2 * half] = nim
            return nre, nim

        sre, sim = lax.fori_loop(0, steps, step,
                                 (st[c, :, 0:half], st[c, :, half:2 * half]), unroll=True)
        st[c, :, 0:half] = sre
        st[c, :, half:2 * half] = sim
        ys.append(_dot(xs[c].astype(BF16), wc_ref[c]))

    u = jnp.concatenate([uin[c] for c in range(SSM_LANE_GROUPS)], axis=1)
    y = jnp.concatenate(ys, axis=1) + d_ref[...] * u
    gl = jax.nn.gelu(y)
    z = _dot(gl.astype(BF16), wglu_ref[...]) + bglu_ref[...]
    out = _rms(gl * jax.nn.sigmoid(z), g_ref[...])
    for c in range(SSM_LANE_GROUPS):
        res[c] = out[:, c * lanes:(c + 1) * lanes]
    for b in range(bsz):
        for c in range(SSM_LANE_GROUPS):
            o_ref[b, :, c * lanes:(c + 1) * lanes] = (
                res[c, pl.ds(b, steps, stride=bsz), :].astype(BF16))


def _ssm(u, wb, wc, lam, dskip, wglu, bglu, g, *, steps):
    bsz, seq, width = u.shape
    r = steps * bsz
    const = lambda shape: pl.BlockSpec(shape, lambda i: (0,) * len(shape),
                                       pipeline_mode=pl.Buffered(1))
    return pl.pallas_call(
        functools.partial(_ssm_kernel, steps=steps),
        grid=(seq // steps,),
        in_specs=[
            pl.BlockSpec((bsz, steps, width), lambda i: (0, i, 0)),
            const(wb.shape), const(wc.shape), const(lam.shape), const(dskip.shape),
            const(wglu.shape), const(bglu.shape), const(g.shape),
        ],
        out_specs=pl.BlockSpec((bsz, steps, width), lambda i: (0, i, 0)),
        out_shape=jax.ShapeDtypeStruct((bsz, seq, width), BF16),
        scratch_shapes=[
            pltpu.VMEM((SSM_LANE_GROUPS, r, V7X_LANES), F32),
            pltpu.VMEM((SSM_LANE_GROUPS, r, V7X_LANES), F32),
            pltpu.VMEM((SSM_LANE_GROUPS, r, wb.shape[2]), F32),
            pltpu.VMEM((SSM_LANE_GROUPS, V7X_SUBLANES, wb.shape[2]), F32),
        ],
        compiler_params=pltpu.CompilerParams(
            dimension_semantics=("arbitrary",),
            vmem_limit_bytes=V7X_VMEM_LIMIT_BYTES),
        name="s5ssm",
    )(u, wb, wc, lam, dskip, wglu, bglu, g)


def _ffn_kernel(x_ref, att_ref, ssm_ref, woa_ref, wos_ref, gf_ref, wg_ref, wu_ref, wd_ref,
                gl_ref, o_ref):
    x1 = x_ref[0] + _dot(att_ref[0], woa_ref[...]) + _dot(ssm_ref[0], wos_ref[...])
    h = _rms(x1, gf_ref[...]).astype(BF16)
    a = (jax.nn.silu(_dot(h, wg_ref[...])) * _dot(h, wu_ref[...])).astype(BF16)
    o_ref[0] = _rms(x1 + _dot(a, wd_ref[...]), gl_ref[...])


def _ffn(x, att, ssm, woa, wos, gf, wg, wu, wd, gl, *, tm):
    bsz, seq, d = x.shape
    const = lambda shape: pl.BlockSpec(shape, lambda b, s: (0,) * len(shape),
                                       pipeline_mode=pl.Buffered(1))
    return pl.pallas_call(
        _ffn_kernel,
        grid=(bsz, seq // tm),
        in_specs=[
            pl.BlockSpec((1, tm, d), lambda b, s: (b, s, 0)),
            pl.BlockSpec((1, tm, ATT_WIDTH), lambda b, s: (b, s, 0)),
            pl.BlockSpec((1, tm, SSM_WIDTH), lambda b, s: (b, s, 0)),
            const(woa.shape), const(wos.shape), const(gf.shape),
            const(wg.shape), const(wu.shape), const(wd.shape), const(gl.shape),
        ],
        out_specs=pl.BlockSpec((1, tm, d), lambda b, s: (b, s, 0)),
        out_shape=jax.ShapeDtypeStruct((bsz, seq, d), x.dtype),
        compiler_params=pltpu.CompilerParams(
            dimension_semantics=("arbitrary", "arbitrary"),
            vmem_limit_bytes=V7X_VMEM_LIMIT_BYTES),
        name="outproj_ffn",
    )(x, att, ssm, woa, wos, gf, wg, wu, wd, gl)


def _t5_bucket(n):
    max_exact = REL_BUCKETS // 2
    nf = jnp.maximum(n, 1).astype(F32)
    large = max_exact + (jnp.log(nf / max_exact) / math.log(REL_MAX_DIST / max_exact)
                         * (REL_BUCKETS - max_exact)).astype(jnp.int32)
    large = jnp.minimum(large, REL_BUCKETS - 1)
    return jnp.where(n < max_exact, n, large)


def _bias_tiles(rel_bias, tq, tk):
    assert tk >= REL_MAX_DIST
    ratio = tq // tk
    nd = tq + tk + 1
    neg = ratio * tk
    table = rel_bias.astype(F32)
    by_dist = table[_t5_bucket(jnp.arange(nd))]
    far = table[_t5_bucket(jnp.full((), REL_MAX_DIST, jnp.int32))]
    by_dist = (by_dist - far[None, :]) * LOG2E
    by_dist = jnp.concatenate([jnp.full((neg, ATT_HEADS), MASK_VALUE, F32), by_dist], axis=0).T
    m = tk + tq
    tiles = []
    for r in range(ratio + 1):
        start = neg - (tk - 1) - (r - 1) * tk
        c = by_dist[:, start:start + m]
        t = jnp.broadcast_to(c[:, None, :], (ATT_HEADS, tk, m)).reshape(ATT_HEADS, tk * m)
        t = t[:, :tk * (m - 1)].reshape(ATT_HEADS, tk, m - 1)
        tiles.append(t[:, :, tk - 1:tk - 1 + tq])
    tiles = jnp.stack(tiles, axis=1)
    tiles = tiles.reshape(ATT_HEADS, ratio + 1, tk, tq // ATT_COLS, ATT_COLS)
    return jnp.transpose(tiles, (0, 1, 3, 2, 4))


def _ssm_params(A_re, A_im, log_dt, B_re, B_im, C_re, C_im):
    lam = lax.complex(A_re.astype(F32), A_im.astype(F32))
    dt = jnp.exp(log_dt.astype(F32))[:, None]
    lam_bar = jnp.exp(lam * dt)
    b_bar = ((lam_bar - 1.0) / lam)[:, :, None] * lax.complex(B_re.astype(F32), B_im.astype(F32))
    nlg = SSM_LANE_GROUPS
    gpl = SSM_GROUPS // nlg
    eye = jnp.eye(gpl, dtype=F32)

    def in_blocks(w):
        w = w.reshape(nlg, gpl, SSM_STATE, SSM_GROUP)
        return jnp.einsum('kgnc,gh->kgchn', w, eye).reshape(nlg, gpl * SSM_GROUP, gpl * SSM_STATE)

    def out_blocks(w):
        w = w.reshape(nlg, gpl, SSM_GROUP, SSM_STATE)
        return jnp.einsum('kgcn,gh->kgnhc', w, eye).reshape(nlg, gpl * SSM_STATE, gpl * SSM_GROUP)

    wb = jnp.concatenate([in_blocks(b_bar.real), in_blocks(b_bar.imag)], axis=2).astype(BF16)
    wc = jnp.concatenate([out_blocks(C_re.astype(F32)), out_blocks(-C_im.astype(F32))],
                         axis=1).astype(BF16)
    lam_ri = jnp.concatenate([lam_bar.real.reshape(nlg, 1, -1), lam_bar.imag.reshape(nlg, 1, -1)],
                             axis=2)
    lam_ri = jnp.broadcast_to(lam_ri, (nlg, V7X_SUBLANES, lam_ri.shape[2]))
    return wb, wc, lam_ri


def kernel(x, norm_mix_g, w_in, lambda_q1, lambda_k1, lambda_q2, lambda_k2, subln_g, rel_bias,
           A_re, A_im, log_dt, B_re, B_im, C_re, C_im, D_skip, w_glu, b_glu, ssm_norm_g, w_out,
           norm_ffn_g, w_gate, w_up, w_down, norm_final_g):
    bsz, seq, d = x.shape
    assert d == D_MODEL and bsz == V7X_SUBLANES
    ts, tq, tk, tm = PROJ_ROWS, ATT_TQ, ATT_TK, FFN_ROWS
    assert seq % ts == 0 and seq % tq == 0 and seq % tm == 0 and seq % SSM_STEPS == 0
    assert ts % tk == 0 and ts % tq == 0 and tq == tk
    l = 0
    row = lambda v: v.astype(F32).reshape(1, -1)

    w = w_in[l].astype(F32)
    qk_scale = ATT_QK_DIM ** -0.5 * LOG2E
    wqT = (w[:, 0:QK_COLS] * qk_scale).T.astype(BF16)
    wk = w[:, QK_COLS:2 * QK_COLS].astype(BF16)
    wvT = w[:, 2 * QK_COLS:2 * QK_COLS + ATT_WIDTH].T.astype(BF16)
    wu = w[:, 2 * QK_COLS + ATT_WIDTH:].astype(BF16)
    lam_init = 0.8 - 0.6 * math.exp(-0.3 * l)
    lam = (jnp.exp(jnp.sum(lambda_q1[l].astype(F32) * lambda_k1[l].astype(F32)))
           - jnp.exp(jnp.sum(lambda_q2[l].astype(F32) * lambda_k2[l].astype(F32)))
           + lam_init).reshape(1, 1)
    bias = _bias_tiles(rel_bias, tq, tk)
    wb, wc, lam_ri = _ssm_params(A_re[l], A_im[l], log_dt[l], B_re[l], B_im[l], C_re[l], C_im[l])

    qT, k, vT, u = _inproj(x, row(norm_mix_g[l]), wqT, wk, wvT, wu, ts=ts, tq=tq, tk=tk)
    att = _attention(lam, qT, k, vT, bias, subln_g[l].astype(F32).reshape(-1, 1),
                     tq=tq, tk=tk, out_scale=1.0 - lam_init)
    ssm = _ssm(u, wb, wc, lam_ri, row(D_skip[l]),
               w_glu[l].astype(BF16), row(b_glu[l]), row(ssm_norm_g[l]), steps=SSM_STEPS)

    wo = w_out[l].astype(BF16)
    return _ffn(x, att, ssm, wo[0:ATT_WIDTH], wo[ATT_WIDTH:],
                row(norm_ffn_g[l]), w_gate[l].astype(BF16), w_up[l].astype(BF16),
                w_down[l].astype(BF16), row(norm_final_g), tm=tm)
```

```python
import functools
import math

import jax
import jax.numpy as jnp
from jax import lax
from jax.experimental import pallas as pl
from jax.experimental.pallas import tpu as pltpu

F32 = jnp.float32
BF16 = jnp.bfloat16

D_MODEL = 1024
ATT_WIDTH = 512
SSM_WIDTH = 512
ATT_HEADS = 4
ATT_QK_DIM = 64
ATT_V_DIM = 128
ATT_V_ROWS = ATT_V_DIM + 16
QK_COLS = 512
SSM_GROUP = 16
SSM_GROUPS = 32
SSM_STATE = 64
REL_BUCKETS = 32
REL_MAX_DIST = 128
EPS = 1e-6
LOG2E = math.log2(math.e)
MASK_VALUE = -1e30

V7X_SUBLANES = 8
V7X_LANES = 128
V7X_VMEM_LIMIT_BYTES = 56 * 1024 * 1024

PROJ_ROWS = 1024
ATT_TQ = 512
ATT_TK = 512
ATT_COLS = 256
SSM_STEPS = 64
SSM_LANE_GROUPS = 4
FFN_ROWS = 512

_NT = (((1,), (1,)), ((), ()))


def _dot(a, b):
    return jnp.dot(a, b, preferred_element_type=F32)


def _rms(x, g):
    ms = jnp.mean(x * x, axis=-1, keepdims=True)
    return x * lax.rsqrt(ms + EPS) * g


def _inproj_kernel(x_ref, g_ref, wqT_ref, wk_ref, wvT_ref, wu_ref,
                   qT_ref, k_ref, vT_ref, u_ref, *, tq, tk):
    h = _rms(x_ref[0], g_ref[...]).astype(BF16)
    qT = lax.dot_general(wqT_ref[...], h, _NT, preferred_element_type=F32)
    qT = qT.astype(BF16)
    for c in range(qT_ref.shape[1]):
        for hd in range(ATT_HEADS):
            qT_ref[0, c, hd] = qT[hd * 2 * ATT_QK_DIM:(hd + 1) * 2 * ATT_QK_DIM,
                                  c * tq:(c + 1) * tq]
    k_ref[0] = _dot(h, wk_ref[...]).astype(BF16)
    vT = lax.dot_general(wvT_ref[...], h, _NT, preferred_element_type=F32).astype(BF16)
    ones = jnp.ones((ATT_V_ROWS - ATT_V_DIM, tk), BF16)
    for c in range(vT_ref.shape[1]):
        for hd in range(ATT_HEADS):
            r0 = hd * ATT_V_ROWS
            vT_ref[0, c, r0:r0 + ATT_V_DIM, :] = vT[hd * ATT_V_DIM:(hd + 1) * ATT_V_DIM,
                                                    c * tk:(c + 1) * tk]
            vT_ref[0, c, r0 + ATT_V_DIM:r0 + ATT_V_ROWS, :] = ones
    u_ref[0] = _dot(h, wu_ref[...])


def _inproj(x, g, wqT, wk, wvT, wu, *, ts, tq, tk):
    bsz, seq, d = x.shape
    ns = seq // ts
    const = lambda shape: pl.BlockSpec(shape, lambda b, s: (0,) * len(shape),
                                       pipeline_mode=pl.Buffered(1))
    return pl.pallas_call(
        functools.partial(_inproj_kernel, tq=tq, tk=tk),
        grid=(bsz, ns),
        in_specs=[
            pl.BlockSpec((1, ts, d), lambda b, s: (b, s, 0)),
            const((1, d)),
            const((QK_COLS, d)),
            const((d, QK_COLS)),
            const((ATT_WIDTH, d)),
            const((d, SSM_WIDTH)),
        ],
        out_specs=[
            pl.BlockSpec((1, ts // tq, ATT_HEADS, 2 * ATT_QK_DIM, tq),
                         lambda b, s: (b, s, 0, 0, 0)),
            pl.BlockSpec((1, ts, QK_COLS), lambda b, s: (b, s, 0)),
            pl.BlockSpec((1, ts // tk, ATT_HEADS * ATT_V_ROWS, tk), lambda b, s: (b, s, 0, 0)),
            pl.BlockSpec((1, ts, SSM_WIDTH), lambda b, s: (b, s, 0)),
        ],
        out_shape=[
            jax.ShapeDtypeStruct((bsz, seq // tq, ATT_HEADS, 2 * ATT_QK_DIM, tq), BF16),
            jax.ShapeDtypeStruct((bsz, seq, QK_COLS), BF16),
            jax.ShapeDtypeStruct((bsz, seq // tk, ATT_HEADS * ATT_V_ROWS, tk), BF16),
            jax.ShapeDtypeStruct((bsz, seq, SSM_WIDTH), F32),
        ],
        compiler_params=pltpu.CompilerParams(
            dimension_semantics=("arbitrary", "arbitrary"),
            vmem_limit_bytes=V7X_VMEM_LIMIT_BYTES),
        name="inproj",
    )(x, g, wqT, wk, wvT, wu)


def _attn_kernel(lam_ref, qT_ref, k_ref, vT_ref, bias_ref, g_ref, o_ref,
                 qbd, s_a, s_b, mx_a, mx_b, m_s, acc_s, *, tq, tk, out_scale):
    assert tq == tk
    nq = qT_ref.shape[1]
    dk = ATT_QK_DIM
    a, b = bufs = ((s_a, mx_a), (s_b, mx_b))
    nblk = 2 * tq // ATT_COLS

    def q_init(qi):
        qbd[...] = jnp.zeros_like(qbd)
        for c0 in range(0, tq, ATT_COLS):
            qbd[c0 // ATT_COLS, 0:dk, :] = qT_ref[0, qi, 0, 0:dk, c0:c0 + ATT_COLS]
            qbd[(tq + c0) // ATT_COLS, dk:2 * dk, :] = qT_ref[0, qi, 0, dk:2 * dk,
                                                              c0:c0 + ATT_COLS]
        m_s[...] = jnp.full_like(m_s, -jnp.inf)
        acc_s[...] = jnp.zeros_like(acc_s)

    def finalize(qi):
        lam = lam_ref[0, 0]
        acc = jnp.concatenate([acc_s[cb, 0:ATT_V_DIM, :] for cb in range(nblk)], axis=1)
        l = jnp.concatenate([acc_s[cb, ATT_V_DIM:ATT_V_DIM + 1, :]
                             for cb in range(nblk)], axis=1)
        o = acc[:, 0:tq] / l[:, 0:tq] - lam * (acc[:, tq:2 * tq] / l[:, tq:2 * tq])
        ms = jnp.mean(o * o, axis=0, keepdims=True)
        o = o * lax.rsqrt(ms + EPS) * g_ref[...] * out_scale
        o_ref[0, pl.ds(pl.multiple_of(qi * tq, tq), tq), :] = o.T.astype(BF16)

    def live_rows(r, c0):
        if r is None or r == 0:
            return tk
        return max(0, min(tk, c0 % tq + ATT_COLS - (r - 1) * tk))

    def logits(tile, bufs2):
        j, r = tile
        buf, mx = bufs2
        r0 = pl.multiple_of(j * tk, tk)
        for c0 in range(0, 2 * tq, ATT_COLS):
            rows = live_rows(r, c0)
            if rows:
                s = _dot(k_ref[0, pl.ds(r0, rows), :], qbd[c0 // ATT_COLS])
                if r is not None:
                    s = s + bias_ref[0, r, (c0 % tq) // ATT_COLS, 0:rows, :]
                buf[c0 // ATT_COLS, 0:rows, :] = s
                mx[:, c0:c0 + ATT_COLS] = jnp.max(s, axis=0, keepdims=True)

    def softmax_pv(tile, bufs2):
        j, r = tile
        buf, mx = bufs2
        for c0 in range(0, 2 * tq, ATT_COLS):
            rows = live_rows(r, c0)
            if rows == 0:
                continue
            cols = slice(c0, c0 + ATT_COLS)
            m_old = m_s[:, cols]
            m_new = jnp.maximum(m_old, mx[:, cols])
            alpha = jnp.exp2(m_old - m_new)
            p = jnp.exp2(buf[c0 // ATT_COLS, 0:rows, :] - m_new)
            cb = c0 // ATT_COLS
            acc_s[cb] = alpha * acc_s[cb] + _dot(vT_ref[0, j, :, 0:rows], p.astype(BF16))
            m_s[:, cols] = m_new

    def run_tiles(tiles, preloaded):
        if not preloaded:
            logits(tiles[0], bufs[0])
        for i, tile in enumerate(tiles):
            if i + 1 < len(tiles):
                logits(tiles[i + 1], bufs[(i + 1) % 2])
            softmax_pv(tile, bufs[i % 2])

    q_init(0)
    run_tiles([(0, 1)], preloaded=False)
    finalize(0)
    q_init(1)
    run_tiles([(0, 0), (1, 1)], preloaded=False)

    def q_block(qi, carry):
        finalize(qi - 1)
        q_init(qi)
        logits((0, None), a)

        def quad(g, c):
            for i in range(4):
                logits((4 * g + i + 1, None), bufs[(i + 1) % 2])
                softmax_pv((4 * g + i, None), bufs[i % 2])
            return c

        nquads = (qi - 2) // 4
        lax.fori_loop(0, nquads, quad, 0)
        t = 4 * nquads

        for rem in range(2, 6):
            @pl.when(qi - t == rem)
            def _(rem=rem):
                far = [(t + i, None) for i in range(rem - 1)]
                run_tiles(far + [(t + rem - 1, 0), (qi, 1)], preloaded=True)

        return carry

    lax.fori_loop(2, nq, q_block, 0)
    finalize(nq - 1)


def _attention(lam, qT, k, vT, bias, g, *, tq, tk, out_scale):
    bsz, seq, _ = k.shape
    nq = seq // tq
    assert tq == tk and nq >= 2
    return pl.pallas_call(
        functools.partial(_attn_kernel, tq=tq, tk=tk, out_scale=out_scale),
        grid=(bsz, ATT_HEADS),
        in_specs=[
            pl.BlockSpec(memory_space=pltpu.SMEM),
            pl.BlockSpec((1, nq, 1, 2 * ATT_QK_DIM, tq), lambda b, h: (b, 0, h, 0, 0)),
            pl.BlockSpec((1, seq, 2 * ATT_QK_DIM), lambda b, h: (b, 0, h)),
            pl.BlockSpec((1, seq // tk, ATT_V_ROWS, tk), lambda b, h: (b, 0, h, 0)),
            pl.BlockSpec((1, 2, tq // ATT_COLS, tk, ATT_COLS), lambda b, h: (h, 0, 0, 0, 0)),
            pl.BlockSpec((ATT_V_DIM, 1), lambda b, h: (0, 0)),
        ],
        out_specs=pl.BlockSpec((1, seq, ATT_V_DIM), lambda b, h: (b, 0, h)),
        out_shape=jax.ShapeDtypeStruct((bsz, seq, ATT_WIDTH), BF16),
        scratch_shapes=[
            pltpu.VMEM((2 * tq // ATT_COLS, 2 * ATT_QK_DIM, ATT_COLS), BF16),
            pltpu.VMEM((2 * tq // ATT_COLS, tk, ATT_COLS), F32),
            pltpu.VMEM((2 * tq // ATT_COLS, tk, ATT_COLS), F32),
            pltpu.VMEM((1, 2 * tq), F32),
            pltpu.VMEM((1, 2 * tq), F32),
            pltpu.VMEM((1, 2 * tq), F32),
            pltpu.VMEM((2 * tq // ATT_COLS, ATT_V_ROWS, ATT_COLS), F32),
        ],
        compiler_params=pltpu.CompilerParams(
            dimension_semantics=("arbitrary", "arbitrary"),
            vmem_limit_bytes=V7X_VMEM_LIMIT_BYTES),
        name="diffattn",
    )(lam, qT, k, vT, bias, g)


def _ssm_kernel(u_ref, wb_ref, wc_ref, lam_ref, d_ref, wglu_ref, bglu_ref, g_ref,
                o_ref, uin, res, xs, st, *, steps):
    rows = V7X_SUBLANES
    half = xs.shape[2] // 2
    lanes = V7X_LANES
    bsz = u_ref.shape[0]

    @pl.when(pl.program_id(0) == 0)
    def _():
        st[...] = jnp.zeros_like(st)

    for b in range(bsz):
        for c in range(SSM_LANE_GROUPS):
            uin[c, pl.ds(b, steps, stride=bsz), :] = u_ref[b, :, c * lanes:(c + 1) * lanes]

    ys = []
    for c in range(SSM_LANE_GROUPS):
        xs[c] = _dot(uin[c].astype(BF16), wb_ref[c])
        lre = lam_ref[c, :, 0:half]
        lim = lam_ref[c, :, half:2 * half]

        def step(t, carry, c=c, lre=lre, lim=lim):
            sre, sim = carry
            r = pl.multiple_of(t * rows, rows)
            bre = xs[c, pl.ds(r, rows), 0:half]
            bim = xs[c, pl.ds(r, rows), half:2 * half]
            nre = lre * sre - lim * sim + bre
            nim = lre * sim + lim * sre + bim
            xs[c, pl.ds(r, rows), 0:half] = nre
            xs[c, pl.ds(r, rows), half:2 * half] = nim
            return nre, nim

        sre, sim = lax.fori_loop(0, steps, step,
                                 (st[c, :, 0:half], st[c, :, half:2 * half]), unroll=True)
        st[c, :, 0:half] = sre
        st[c, :, half:2 * half] = sim
        ys.append(_dot(xs[c].astype(BF16), wc_ref[c]))

    u = jnp.concatenate([uin[c] for c in range(SSM_LANE_GROUPS)], axis=1)
    y = jnp.concatenate(ys, axis=1) + d_ref[...] * u
    gl = jax.nn.gelu(y)
    z = _dot(gl.astype(BF16), wglu_ref[...]) + bglu_ref[...]
    out = _rms(gl * jax.nn.sigmoid(z), g_ref[...])
    for c in range(SSM_LANE_GROUPS):
        res[c] = out[:, c * lanes:(c + 1) * lanes]
    for b in range(bsz):
        for c in range(SSM_LANE_GROUPS):
            o_ref[b, :, c * lanes:(c + 1) * lanes] = (
                res[c, pl.ds(b, steps, stride=bsz), :].astype(BF16))


def _ssm(u, wb, wc, lam, dskip, wglu, bglu, g, *, steps):
    bsz, seq, width = u.shape
    r = steps * bsz
    const = lambda shape: pl.BlockSpec(shape, lambda i: (0,) * len(shape),
                                       pipeline_mode=pl.Buffered(1))
    return pl.pallas_call(
        functools.partial(_ssm_kernel, steps=steps),
        grid=(seq // steps,),
        in_specs=[
            pl.BlockSpec((bsz, steps, width), lambda i: (0, i, 0)),
            const(wb.shape), const(wc.shape), const(lam.shape), const(dskip.shape),
            const(wglu.shape), const(bglu.shape), const(g.shape),
        ],
        out_specs=pl.BlockSpec((bsz, steps, width), lambda i: (0, i, 0)),
        out_shape=jax.ShapeDtypeStruct((bsz, seq, width), BF16),
        scratch_shapes=[
            pltpu.VMEM((SSM_LANE_GROUPS, r, V7X_LANES), F32),
            pltpu.VMEM((SSM_LANE_GROUPS, r, V7X_LANES), F32),
            pltpu.VMEM((SSM_LANE_GROUPS, r, wb.shape[2]), F32),
            pltpu.VMEM((SSM_LANE_GROUPS, V7X_SUBLANES, wb.shape[2]), F32),
        ],
        compiler_params=pltpu.CompilerParams(
            dimension_semantics=("arbitrary",),
            vmem_limit_bytes=V7X_VMEM_LIMIT_BYTES),
        name="s5ssm",
    )(u, wb, wc, lam, dskip, wglu, bglu, g)


def _ffn_kernel(x_ref, att_ref, ssm_ref, woa_ref, wos_ref, gf_ref, wg_ref, wu_ref, wd_ref,
                gl_ref, o_ref):
    x1 = x_ref[0] + _dot(att_ref[0], woa_ref[...]) + _dot(ssm_ref[0], wos_ref[...])
    h = _rms(x1, gf_ref[...]).astype(BF16)
    a = (jax.nn.silu(_dot(h, wg_ref[...])) * _dot(h, wu_ref[...])).astype(BF16)
    o_ref[0] = _rms(x1 + _dot(a, wd_ref[...]), gl_ref[...])


def _ffn(x, att, ssm, woa, wos, gf, wg, wu, wd, gl, *, tm):
    bsz, seq, d = x.shape
    const = lambda shape: pl.BlockSpec(shape, lambda b, s: (0,) * len(shape),
                                       pipeline_mode=pl.Buffered(1))
    return pl.pallas_call(
        _ffn_kernel,
        grid=(bsz, seq // tm),
        in_specs=[
            pl.BlockSpec((1, tm, d), lambda b, s: (b, s, 0)),
            pl.BlockSpec((1, tm, ATT_WIDTH), lambda b, s: (b, s, 0)),
            pl.BlockSpec((1, tm, SSM_WIDTH), lambda b, s: (b, s, 0)),
            const(woa.shape), const(wos.shape), const(gf.shape),
            const(wg.shape), const(wu.shape), const(wd.shape), const(gl.shape),
        ],
        out_specs=pl.BlockSpec((1, tm, d), lambda b, s: (b, s, 0)),
        out_shape=jax.ShapeDtypeStruct((bsz, seq, d), x.dtype),
        compiler_params=pltpu.CompilerParams(
            dimension_semantics=("arbitrary", "arbitrary"),
            vmem_limit_bytes=V7X_VMEM_LIMIT_BYTES),
        name="outproj_ffn",
    )(x, att, ssm, woa, wos, gf, wg, wu, wd, gl)


def _t5_bucket(n):
    max_exact = REL_BUCKETS // 2
    nf = jnp.maximum(n, 1).astype(F32)
    large = max_exact + (jnp.log(nf / max_exact) / math.log(REL_MAX_DIST / max_exact)
                         * (REL_BUCKETS - max_exact)).astype(jnp.int32)
    large = jnp.minimum(large, REL_BUCKETS - 1)
    return jnp.where(n < max_exact, n, large)


def _bias_tiles(rel_bias, tq, tk):
    assert tk >= REL_MAX_DIST
    ratio = tq // tk
    nd = tq + tk + 1
    neg = ratio * tk
    table = rel_bias.astype(F32)
    by_dist = table[_t5_bucket(jnp.arange(nd))]
    far = table[_t5_bucket(jnp.full((), REL_MAX_DIST, jnp.int32))]
    by_dist = (by_dist - far[None, :]) * LOG2E
    by_dist = jnp.concatenate([jnp.full((neg, ATT_HEADS), MASK_VALUE, F32), by_dist], axis=0).T
    m = tk + tq
    tiles = []
    for r in range(ratio + 1):
        start = neg - (tk - 1) - (r - 1) * tk
        c = by_dist[:, start:start + m]
        t = jnp.broadcast_to(c[:, None, :], (ATT_HEADS, tk, m)).reshape(ATT_HEADS, tk * m)
        t = t[:, :tk * (m - 1)].reshape(ATT_HEADS, tk, m - 1)
        tiles.append(t[:, :, tk - 1:tk - 1 + tq])
    tiles = jnp.stack(tiles, axis=1)
    tiles = tiles.reshape(ATT_HEADS, ratio + 1, tk, tq // ATT_COLS, ATT_COLS)
    return jnp.transpose(tiles, (0, 1, 3, 2, 4))


def _ssm_params(A_re, A_im, log_dt, B_re, B_im, C_re, C_im):
    lam = lax.complex(A_re.astype(F32), A_im.astype(F32))
    dt = jnp.exp(log_dt.astype(F32))[:, None]
    lam_bar = jnp.exp(lam * dt)
    b_bar = ((lam_bar - 1.0) / lam)[:, :, None] * lax.complex(B_re.astype(F32), B_im.astype(F32))
    nlg = SSM_LANE_GROUPS
    gpl = SSM_GROUPS // nlg
    eye = jnp.eye(gpl, dtype=F32)

    def in_blocks(w):
        w = w.reshape(nlg, gpl, SSM_STATE, SSM_GROUP)
        return jnp.einsum('kgnc,gh->kgchn', w, eye).reshape(nlg, gpl * SSM_GROUP, gpl * SSM_STATE)

    def out_blocks(w):
        w = w.reshape(nlg, gpl, SSM_GROUP, SSM_STATE)
        return jnp.einsum('kgcn,gh->kgnhc', w, eye).reshape(nlg, gpl * SSM_STATE, gpl * SSM_GROUP)

    wb = jnp.concatenate([in_blocks(b_bar.real), in_blocks(b_bar.imag)], axis=2).astype(BF16)
    wc = jnp.concatenate([out_blocks(C_re.astype(F32)), out_blocks(-C_im.astype(F32))],
                         axis=1).astype(BF16)
    lam_ri = jnp.concatenate([lam_bar.real.reshape(nlg, 1, -1), lam_bar.imag.reshape(nlg, 1, -1)],
                             axis=2)
    lam_ri = jnp.broadcast_to(lam_ri, (nlg, V7X_SUBLANES, lam_ri.shape[2]))
    return wb, wc, lam_ri


def kernel(x, norm_mix_g, w_in, lambda_q1, lambda_k1, lambda_q2, lambda_k2, subln_g, rel_bias,
           A_re, A_im, log_dt, B_re, B_im, C_re, C_im, D_skip, w_glu, b_glu, ssm_norm_g, w_out,
           norm_ffn_g, w_gate, w_up, w_down, norm_final_g):
    bsz, seq, d = x.shape
    assert d == D_MODEL and bsz == V7X_SUBLANES
    ts, tq, tk, tm = PROJ_ROWS, ATT_TQ, ATT_TK, FFN_ROWS
    assert seq % ts == 0 and seq % tq == 0 and seq % tm == 0 and seq % SSM_STEPS == 0
    assert ts % tk == 0 and ts % tq == 0 and tq == tk
    l = 0
    row = lambda v: v.astype(F32).reshape(1, -1)

    w = w_in[l].astype(F32)
    qk_scale = ATT_QK_DIM ** -0.5 * LOG2E
    wqT = (w[:, 0:QK_COLS] * qk_scale).T.astype(BF16)
    wk = w[:, QK_COLS:2 * QK_COLS].astype(BF16)
    wvT = w[:, 2 * QK_COLS:2 * QK_COLS + ATT_WIDTH].T.astype(BF16)
    wu = w[:, 2 * QK_COLS + ATT_WIDTH:].astype(BF16)
    lam_init = 0.8 - 0.6 * math.exp(-0.3 * l)
    lam = (jnp.exp(jnp.sum(lambda_q1[l].astype(F32) * lambda_k1[l].astype(F32)))
           - jnp.exp(jnp.sum(lambda_q2[l].astype(F32) * lambda_k2[l].astype(F32)))
           + lam_init).reshape(1, 1)
    bias = _bias_tiles(rel_bias, tq, tk)
    wb, wc, lam_ri = _ssm_params(A_re[l], A_im[l], log_dt[l], B_re[l], B_im[l], C_re[l], C_im[l])

    qT, k, vT, u = _inproj(x, row(norm_mix_g[l]), wqT, wk, wvT, wu, ts=ts, tq=tq, tk=tk)
    att = _attention(lam, qT, k, vT, bias, subln_g[l].astype(F32).reshape(-1, 1),
                     tq=tq, tk=tk, out_scale=1.0 - lam_init)
    ssm = _ssm(u, wb, wc, lam_ri, row(D_skip[l]),
               w_glu[l].astype(BF16), row(b_glu[l]), row(ssm_norm_g[l]), steps=SSM_STEPS)

    wo = w_out[l].astype(BF16)
    return _ffn(x, att, ssm, wo[0:ATT_WIDTH], wo[ATT_WIDTH:],
                row(norm_ffn_g[l]), w_gate[l].astype(BF16), w_up[l].astype(BF16),
                w_down[l].astype(BF16), row(norm_final_g), tm=tm)
```

```python
import functools
import math

import jax
import jax.numpy as jnp
from jax import lax
from jax.experimental import pallas as pl
from jax.experimental.pallas import tpu as pltpu

F32 = jnp.float32
BF16 = jnp.bfloat16

D_MODEL = 1024
ATT_WIDTH = 512
SSM_WIDTH = 512
ATT_HEADS = 4
ATT_QK_DIM = 64
ATT_V_DIM = 128
ATT_V_ROWS = ATT_V_DIM + 16
QK_COLS = 512
SSM_GROUP = 16
SSM_GROUPS = 32
SSM_STATE = 64
REL_BUCKETS = 32
REL_MAX_DIST = 128
EPS = 1e-6
LOG2E = math.log2(math.e)
MASK_VALUE = -1e30

V7X_SUBLANES = 8
V7X_LANES = 128
V7X_VMEM_LIMIT_BYTES = 56 * 1024 * 1024

PROJ_ROWS = 1024
ATT_TQ = 512
ATT_TK = 512
ATT_COLS = 256
ATT_UNROLL = 8
SSM_STEPS = 64
SSM_LANE_GROUPS = 4
FFN_ROWS = 512

_NT = (((1,), (1,)), ((), ()))


def _dot(a, b):
    return jnp.dot(a, b, preferred_element_type=F32)


def _rms(x, g):
    ms = jnp.mean(x * x, axis=-1, keepdims=True)
    return x * lax.rsqrt(ms + EPS) * g


def _inproj_kernel(x_ref, g_ref, wqT_ref, wk_ref, wvT_ref, wu_ref,
                   qT_ref, k_ref, vT_ref, u_ref, *, tq, tk):
    h = _rms(x_ref[0], g_ref[...]).astype(BF16)
    qT = lax.dot_general(wqT_ref[...], h, _NT, preferred_element_type=F32)
    qT = qT.astype(BF16)
    for c in range(qT_ref.shape[1]):
        for hd in range(ATT_HEADS):
            qT_ref[0, c, hd] = qT[hd * 2 * ATT_QK_DIM:(hd + 1) * 2 * ATT_QK_DIM,
                                  c * tq:(c + 1) * tq]
    k_ref[0] = _dot(h, wk_ref[...]).astype(BF16)
    vT = lax.dot_general(wvT_ref[...], h, _NT, preferred_element_type=F32).astype(BF16)
    ones = jnp.ones((ATT_V_ROWS - ATT_V_DIM, tk), BF16)
    for c in range(vT_ref.shape[1]):
        for hd in range(ATT_HEADS):
            r0 = hd * ATT_V_ROWS
            vT_ref[0, c, r0:r0 + ATT_V_DIM, :] = vT[hd * ATT_V_DIM:(hd + 1) * ATT_V_DIM,
                                                    c * tk:(c + 1) * tk]
            vT_ref[0, c, r0 + ATT_V_DIM:r0 + ATT_V_ROWS, :] = ones
    u_ref[0] = _dot(h, wu_ref[...])


def _inproj(x, g, wqT, wk, wvT, wu, *, ts, tq, tk):
    bsz, seq, d = x.shape
    ns = seq // ts
    const = lambda shape: pl.BlockSpec(shape, lambda b, s: (0,) * len(shape),
                                       pipeline_mode=pl.Buffered(1))
    return pl.pallas_call(
        functools.partial(_inproj_kernel, tq=tq, tk=tk),
        grid=(bsz, ns),
        in_specs=[
            pl.BlockSpec((1, ts, d), lambda b, s: (b, s, 0)),
            const((1, d)),
            const((QK_COLS, d)),
            const((d, QK_COLS)),
            const((ATT_WIDTH, d)),
            const((d, SSM_WIDTH)),
        ],
        out_specs=[
            pl.BlockSpec((1, ts // tq, ATT_HEADS, 2 * ATT_QK_DIM, tq),
                         lambda b, s: (b, s, 0, 0, 0)),
            pl.BlockSpec((1, ts, QK_COLS), lambda b, s: (b, s, 0)),
            pl.BlockSpec((1, ts // tk, ATT_HEADS * ATT_V_ROWS, tk), lambda b, s: (b, s, 0, 0)),
            pl.BlockSpec((1, ts, SSM_WIDTH), lambda b, s: (b, s, 0)),
        ],
        out_shape=[
            jax.ShapeDtypeStruct((bsz, seq // tq, ATT_HEADS, 2 * ATT_QK_DIM, tq), BF16),
            jax.ShapeDtypeStruct((bsz, seq, QK_COLS), BF16),
            jax.ShapeDtypeStruct((bsz, seq // tk, ATT_HEADS * ATT_V_ROWS, tk), BF16),
            jax.ShapeDtypeStruct((bsz, seq, SSM_WIDTH), F32),
        ],
        compiler_params=pltpu.CompilerParams(
            dimension_semantics=("arbitrary", "arbitrary"),
            vmem_limit_bytes=V7X_VMEM_LIMIT_BYTES),
        name="inproj",
    )(x, g, wqT, wk, wvT, wu)


def _attn_kernel(lam_ref, qT_ref, k_ref, vT_ref, bias_ref, g_ref, o_ref,
                 qbd, s_a, s_b, mx_a, mx_b, m_s, acc_s, *, tq, tk, out_scale):
    assert tq == tk
    nq = qT_ref.shape[1]
    dk = ATT_QK_DIM
    a, b = bufs = ((s_a, mx_a), (s_b, mx_b))
    nblk = 2 * tq // ATT_COLS

    def q_init(qi):
        qbd[...] = jnp.zeros_like(qbd)
        for c0 in range(0, tq, ATT_COLS):
            qbd[c0 // ATT_COLS, 0:dk, :] = qT_ref[0, qi, 0, 0:dk, c0:c0 + ATT_COLS]
            qbd[(tq + c0) // ATT_COLS, dk:2 * dk, :] = qT_ref[0, qi, 0, dk:2 * dk,
                                                              c0:c0 + ATT_COLS]
        m_s[...] = jnp.full_like(m_s, -jnp.inf)
        acc_s[...] = jnp.zeros_like(acc_s)

    def finalize(qi):
        lam = lam_ref[0, 0]
        acc = jnp.concatenate([acc_s[cb, 0:ATT_V_DIM, :] for cb in range(nblk)], axis=1)
        l = jnp.concatenate([acc_s[cb, ATT_V_DIM:ATT_V_DIM + 1, :]
                             for cb in range(nblk)], axis=1)
        o = acc[:, 0:tq] / l[:, 0:tq] - lam * (acc[:, tq:2 * tq] / l[:, tq:2 * tq])
        ms = jnp.mean(o * o, axis=0, keepdims=True)
        o = o * lax.rsqrt(ms + EPS) * g_ref[...] * out_scale
        o_ref[0, pl.ds(pl.multiple_of(qi * tq, tq), tq), :] = o.T.astype(BF16)

    def live_rows(r, c0):
        if r is None or r == 0:
            return tk
        return max(0, min(tk, c0 % tq + ATT_COLS - (r - 1) * tk))

    def logits(tile, bufs2):
        j, r = tile
        buf, mx = bufs2
        r0 = pl.multiple_of(j * tk, tk)
        for c0 in range(0, 2 * tq, ATT_COLS):
            rows = live_rows(r, c0)
            if rows:
                s = _dot(k_ref[0, pl.ds(r0, rows), :], qbd[c0 // ATT_COLS])
                if r is not None:
                    s = s + bias_ref[0, r, (c0 % tq) // ATT_COLS, 0:rows, :]
                buf[c0 // ATT_COLS, 0:rows, :] = s
                mx[:, c0:c0 + ATT_COLS] = jnp.max(s, axis=0, keepdims=True)

    def softmax_pv(tile, bufs2):
        j, r = tile
        buf, mx = bufs2
        for c0 in range(0, 2 * tq, ATT_COLS):
            rows = live_rows(r, c0)
            if rows == 0:
                continue
            cols = slice(c0, c0 + ATT_COLS)
            m_old = m_s[:, cols]
            m_new = jnp.maximum(m_old, mx[:, cols])
            alpha = jnp.exp2(m_old - m_new)
            p = jnp.exp2(buf[c0 // ATT_COLS, 0:rows, :] - m_new)
            cb = c0 // ATT_COLS
            acc_s[cb] = alpha * acc_s[cb] + _dot(vT_ref[0, j, :, 0:rows], p.astype(BF16))
            m_s[:, cols] = m_new

    def run_tiles(tiles, preloaded):
        if not preloaded:
            logits(tiles[0], bufs[0])
        for i, tile in enumerate(tiles):
            if i + 1 < len(tiles):
                logits(tiles[i + 1], bufs[(i + 1) % 2])
            softmax_pv(tile, bufs[i % 2])

    q_init(0)
    run_tiles([(0, 1)], preloaded=False)
    finalize(0)
    q_init(1)
    run_tiles([(0, 0), (1, 1)], preloaded=False)

    def q_block(qi, carry):
        finalize(qi - 1)
        q_init(qi)
        logits((0, None), a)

        un = ATT_UNROLL

        def trip(g, c):
            for i in range(un):
                logits((un * g + i + 1, None), bufs[(i + 1) % 2])
                softmax_pv((un * g + i, None), bufs[i % 2])
            return c

        ntrips = (qi - 2) // un
        lax.fori_loop(0, ntrips, trip, 0)
        t = un * ntrips

        for rem in range(2, un + 2):
            @pl.when(qi - t == rem)
            def _(rem=rem):
                far = [(t + i, None) for i in range(rem - 1)]
                run_tiles(far + [(t + rem - 1, 0), (qi, 1)], preloaded=True)

        return carry

    lax.fori_loop(2, nq, q_block, 0)
    finalize(nq - 1)


def _attention(lam, qT, k, vT, bias, g, *, tq, tk, out_scale):
    bsz, seq, _ = k.shape
    nq = seq // tq
    assert tq == tk and nq >= 2
    return pl.pallas_call(
        functools.partial(_attn_kernel, tq=tq, tk=tk, out_scale=out_scale),
        grid=(bsz, ATT_HEADS),
        in_specs=[
            pl.BlockSpec(memory_space=pltpu.SMEM),
            pl.BlockSpec((1, nq, 1, 2 * ATT_QK_DIM, tq), lambda b, h: (b, 0, h, 0, 0)),
            pl.BlockSpec((1, seq, 2 * ATT_QK_DIM), lambda b, h: (b, 0, h)),
            pl.BlockSpec((1, seq // tk, ATT_V_ROWS, tk), lambda b, h: (b, 0, h, 0)),
            pl.BlockSpec((1, 2, tq // ATT_COLS, tk, ATT_COLS), lambda b, h: (h, 0, 0, 0, 0)),
            pl.BlockSpec((ATT_V_DIM, 1), lambda b, h: (0, 0)),
        ],
        out_specs=pl.BlockSpec((1, seq, ATT_V_DIM), lambda b, h: (b, 0, h)),
        out_shape=jax.ShapeDtypeStruct((bsz, seq, ATT_WIDTH), BF16),
        scratch_shapes=[
            pltpu.VMEM((2 * tq // ATT_COLS, 2 * ATT_QK_DIM, ATT_COLS), BF16),
            pltpu.VMEM((2 * tq // ATT_COLS, tk, ATT_COLS), F32),
            pltpu.VMEM((2 * tq // ATT_COLS, tk, ATT_COLS), F32),
            pltpu.VMEM((1, 2 * tq), F32),
            pltpu.VMEM((1, 2 * tq), F32),
            pltpu.VMEM((1, 2 * tq), F32),
            pltpu.VMEM((2 * tq // ATT_COLS, ATT_V_ROWS, ATT_COLS), F32),
        ],
        compiler_params=pltpu.CompilerParams(
            dimension_semantics=("arbitrary", "arbitrary"),
            vmem_limit_bytes=V7X_VMEM_LIMIT_BYTES),
        name="diffattn",
    )(lam, qT, k, vT, bias, g)


def _ssm_kernel(u_ref, wb_ref, wc_ref, lam_ref, d_ref, wglu_ref, bglu_ref, g_ref,
                o_ref, uin, res, xs, st, *, steps):
    rows = V7X_SUBLANES
    half = xs.shape[2] // 2
    lanes = V7X_LANES
    bsz = u_ref.shape[0]

    @pl.when(pl.program_id(0) == 0)
    def _():
        st[...] = jnp.zeros_like(st)

    for b in range(bsz):
        for c in range(SSM_LANE_GROUPS):
            uin[c, pl.ds(b, steps, stride=bsz), :] = u_ref[b, :, c * lanes:(c + 1) * lanes]

    ys = []
    xs[0] = _dot(uin[0].astype(BF16), wb_ref[0])
    for c in range(SSM_LANE_GROUPS):
        if c + 1 < SSM_LANE_GROUPS:
            xs[c + 1] = _dot(uin[c + 1].astype(BF16), wb_ref[c + 1])
        lre = lam_ref[c, :, 0:half]
        lim = lam_ref[c, :, half:2 * half]

        def step(t, carry, c=c, lre=lre, lim=lim):
            sre, sim = carry
            r = pl.multiple_of(t * rows, rows)
            bre = xs[c, pl.ds(r, rows), 0:half]
            bim = xs[c, pl.ds(r, rows), half:2 * half]
            nre = lre * sre - lim * sim + bre
            nim = lre * sim + lim * sre + bim
            xs[c, pl.ds(r, rows), 0:half] = nre
            xs[c, pl.ds(r, rows), half:2 * half] = nim
            return nre, nim

        sre, sim = lax.fori_loop(0, steps, step,
                                 (st[c, :, 0:half], st[c, :, half:2 * half]), unroll=True)
        st[c, :, 0:half] = sre
        st[c, :, half:2 * half] = sim
        ys.append(_dot(xs[c].astype(BF16), wc_ref[c]))

    u = jnp.concatenate([uin[c] for c in range(SSM_LANE_GROUPS)], axis=1)
    y = jnp.concatenate(ys, axis=1) + d_ref[...] * u
    gl = jax.nn.gelu(y)
    z = _dot(gl.astype(BF16), wglu_ref[...]) + bglu_ref[...]
    out = _rms(gl * jax.nn.sigmoid(z), g_ref[...])
    for c in range(SSM_LANE_GROUPS):
        res[c] = out[:, c * lanes:(c + 1) * lanes]
    for b in range(bsz):
        for c in range(SSM_LANE_GROUPS):
            o_ref[b, :, c * lanes:(c + 1) * lanes] = (
                res[c, pl.ds(b, steps, stride=bsz), :].astype(BF16))


def _ssm(u, wb, wc, lam, dskip, wglu, bglu, g, *, steps):
    bsz, seq, width = u.shape
    r = steps * bsz
    const = lambda shape: pl.BlockSpec(shape, lambda i: (0,) * len(shape),
                                       pipeline_mode=pl.Buffered(1))
    return pl.pallas_call(
        functools.partial(_ssm_kernel, steps=steps),
        grid=(seq // steps,),
        in_specs=[
            pl.BlockSpec((bsz, steps, width), lambda i: (0, i, 0)),
            const(wb.shape), const(wc.shape), const(lam.shape), const(dskip.shape),
            const(wglu.shape), const(bglu.shape), const(g.shape),
        ],
        out_specs=pl.BlockSpec((bsz, steps, width), lambda i: (0, i, 0)),
        out_shape=jax.ShapeDtypeStruct((bsz, seq, width), BF16),
        scratch_shapes=[
            pltpu.VMEM((SSM_LANE_GROUPS, r, V7X_LANES), F32),
            pltpu.VMEM((SSM_LANE_GROUPS, r, V7X_LANES), F32),
            pltpu.VMEM((SSM_LANE_GROUPS, r, wb.shape[2]), F32),
            pltpu.VMEM((SSM_LANE_GROUPS, V7X_SUBLANES, wb.shape[2]), F32),
        ],
        compiler_params=pltpu.CompilerParams(
            dimension_semantics=("arbitrary",),
            vmem_limit_bytes=V7X_VMEM_LIMIT_BYTES),
        name="s5ssm",
    )(u, wb, wc, lam, dskip, wglu, bglu, g)


def _ffn_kernel(x_ref, att_ref, ssm_ref, woa_ref, wos_ref, gf_ref, wg_ref, wu_ref, wd_ref,
                gl_ref, o_ref):
    x1 = x_ref[0] + _dot(att_ref[0], woa_ref[...]) + _dot(ssm_ref[0], wos_ref[...])
    h = _rms(x1, gf_ref[...]).astype(BF16)
    a = (jax.nn.silu(_dot(h, wg_ref[...])) * _dot(h, wu_ref[...])).astype(BF16)
    o_ref[0] = _rms(x1 + _dot(a, wd_ref[...]), gl_ref[...])


def _ffn(x, att, ssm, woa, wos, gf, wg, wu, wd, gl, *, tm):
    bsz, seq, d = x.shape
    const = lambda shape: pl.BlockSpec(shape, lambda b, s: (0,) * len(shape),
                                       pipeline_mode=pl.Buffered(1))
    return pl.pallas_call(
        _ffn_kernel,
        grid=(bsz, seq // tm),
        in_specs=[
            pl.BlockSpec((1, tm, d), lambda b, s: (b, s, 0)),
            pl.BlockSpec((1, tm, ATT_WIDTH), lambda b, s: (b, s, 0)),
            pl.BlockSpec((1, tm, SSM_WIDTH), lambda b, s: (b, s, 0)),
            const(woa.shape), const(wos.shape), const(gf.shape),
            const(wg.shape), const(wu.shape), const(wd.shape), const(gl.shape),
        ],
        out_specs=pl.BlockSpec((1, tm, d), lambda b, s: (b, s, 0)),
        out_shape=jax.ShapeDtypeStruct((bsz, seq, d), x.dtype),
        compiler_params=pltpu.CompilerParams(
            dimension_semantics=("arbitrary", "arbitrary"),
            vmem_limit_bytes=V7X_VMEM_LIMIT_BYTES),
        name="outproj_ffn",
    )(x, att, ssm, woa, wos, gf, wg, wu, wd, gl)


def _t5_bucket(n):
    max_exact = REL_BUCKETS // 2
    nf = jnp.maximum(n, 1).astype(F32)
    large = max_exact + (jnp.log(nf / max_exact) / math.log(REL_MAX_DIST / max_exact)
                         * (REL_BUCKETS - max_exact)).astype(jnp.int32)
    large = jnp.minimum(large, REL_BUCKETS - 1)
    return jnp.where(n < max_exact, n, large)


def _bias_tiles(rel_bias, tq, tk):
    assert tk >= REL_MAX_DIST
    ratio = tq // tk
    nd = tq + tk + 1
    neg = ratio * tk
    table = rel_bias.astype(F32)
    by_dist = table[_t5_bucket(jnp.arange(nd))]
    far = table[_t5_bucket(jnp.full((), REL_MAX_DIST, jnp.int32))]
    by_dist = (by_dist - far[None, :]) * LOG2E
    by_dist = jnp.concatenate([jnp.full((neg, ATT_HEADS), MASK_VALUE, F32), by_dist], axis=0).T
    m = tk + tq
    tiles = []
    for r in range(ratio + 1):
        start = neg - (tk - 1) - (r - 1) * tk
        c = by_dist[:, start:start + m]
        t = jnp.broadcast_to(c[:, None, :], (ATT_HEADS, tk, m)).reshape(ATT_HEADS, tk * m)
        t = t[:, :tk * (m - 1)].reshape(ATT_HEADS, tk, m - 1)
        tiles.append(t[:, :, tk - 1:tk - 1 + tq])
    tiles = jnp.stack(tiles, axis=1)
    tiles = tiles.reshape(ATT_HEADS, ratio + 1, tk, tq // ATT_COLS, ATT_COLS)
    return jnp.transpose(tiles, (0, 1, 3, 2, 4))


def _ssm_params(A_re, A_im, log_dt, B_re, B_im, C_re, C_im):
    lam = lax.complex(A_re.astype(F32), A_im.astype(F32))
    dt = jnp.exp(log_dt.astype(F32))[:, None]
    lam_bar = jnp.exp(lam * dt)
    b_bar = ((lam_bar - 1.0) / lam)[:, :, None] * lax.complex(B_re.astype(F32), B_im.astype(F32))
    nlg = SSM_LANE_GROUPS
    gpl = SSM_GROUPS // nlg
    eye = jnp.eye(gpl, dtype=F32)

    def in_blocks(w):
        w = w.reshape(nlg, gpl, SSM_STATE, SSM_GROUP)
        return jnp.einsum('kgnc,gh->kgchn', w, eye).reshape(nlg, gpl * SSM_GROUP, gpl * SSM_STATE)

    def out_blocks(w):
        w = w.reshape(nlg, gpl, SSM_GROUP, SSM_STATE)
        return jnp.einsum('kgcn,gh->kgnhc', w, eye).reshape(nlg, gpl * SSM_STATE, gpl * SSM_GROUP)

    wb = jnp.concatenate([in_blocks(b_bar.real), in_blocks(b_bar.imag)], axis=2).astype(BF16)
    wc = jnp.concatenate([out_blocks(C_re.astype(F32)), out_blocks(-C_im.astype(F32))],
                         axis=1).astype(BF16)
    lam_ri = jnp.concatenate([lam_bar.real.reshape(nlg, 1, -1), lam_bar.imag.reshape(nlg, 1, -1)],
                             axis=2)
    lam_ri = jnp.broadcast_to(lam_ri, (nlg, V7X_SUBLANES, lam_ri.shape[2]))
    return wb, wc, lam_ri


def kernel(x, norm_mix_g, w_in, lambda_q1, lambda_k1, lambda_q2, lambda_k2, subln_g, rel_bias,
           A_re, A_im, log_dt, B_re, B_im, C_re, C_im, D_skip, w_glu, b_glu, ssm_norm_g, w_out,
           norm_ffn_g, w_gate, w_up, w_down, norm_final_g):
    bsz, seq, d = x.shape
    assert d == D_MODEL and bsz == V7X_SUBLANES
    ts, tq, tk, tm = PROJ_ROWS, ATT_TQ, ATT_TK, FFN_ROWS
    assert seq % ts == 0 and seq % tq == 0 and seq % tm == 0 and seq % SSM_STEPS == 0
    assert ts % tk == 0 and ts % tq == 0 and tq == tk
    l = 0
    row = lambda v: v.astype(F32).reshape(1, -1)

    w = w_in[l].astype(F32)
    qk_scale = ATT_QK_DIM ** -0.5 * LOG2E
    wqT = (w[:, 0:QK_COLS] * qk_scale).T.astype(BF16)
    wk = w[:, QK_COLS:2 * QK_COLS].astype(BF16)
    wvT = w[:, 2 * QK_COLS:2 * QK_COLS + ATT_WIDTH].T.astype(BF16)
    wu = w[:, 2 * QK_COLS + ATT_WIDTH:].astype(BF16)
    lam_init = 0.8 - 0.6 * math.exp(-0.3 * l)
    lam = (jnp.exp(jnp.sum(lambda_q1[l].astype(F32) * lambda_k1[l].astype(F32)))
           - jnp.exp(jnp.sum(lambda_q2[l].astype(F32) * lambda_k2[l].astype(F32)))
           + lam_init).reshape(1, 1)
    bias = _bias_tiles(rel_bias, tq, tk)
    wb, wc, lam_ri = _ssm_params(A_re[l], A_im[l], log_dt[l], B_re[l], B_im[l], C_re[l], C_im[l])

    qT, k, vT, u = _inproj(x, row(norm_mix_g[l]), wqT, wk, wvT, wu, ts=ts, tq=tq, tk=tk)
    att = _attention(lam, qT, k, vT, bias, subln_g[l].astype(F32).reshape(-1, 1),
                     tq=tq, tk=tk, out_scale=1.0 - lam_init)
    ssm = _ssm(u, wb, wc, lam_ri, row(D_skip[l]),
               w_glu[l].astype(BF16), row(b_glu[l]), row(ssm_norm_g[l]), steps=SSM_STEPS)

    wo = w_out[l].astype(BF16)
    return _ffn(x, att, ssm, wo[0:ATT_WIDTH], wo[ATT_WIDTH:],
                row(norm_ffn_g[l]), w_gate[l].astype(BF16), w_up[l].astype(BF16),
                w_down[l].astype(BF16), row(norm_final_g), tm=tm)
```

```python
import functools
import math

import jax
import jax.numpy as jnp
from jax import lax
from jax.experimental import pallas as pl
from jax.experimental.pallas import tpu as pltpu

F32 = jnp.float32
BF16 = jnp.bfloat16

D_MODEL = 1024
ATT_WIDTH = 512
SSM_WIDTH = 512
ATT_HEADS = 4
ATT_QK_DIM = 64
ATT_V_DIM = 128
ATT_V_ROWS = ATT_V_DIM + 16
QK_COLS = 512
SSM_GROUP = 16
SSM_GROUPS = 32
SSM_STATE = 64
REL_BUCKETS = 32
REL_MAX_DIST = 128
EPS = 1e-6
LOG2E = math.log2(math.e)
MASK_VALUE = -1e30

V7X_SUBLANES = 8
V7X_LANES = 128
V7X_VMEM_LIMIT_BYTES = 56 * 1024 * 1024

PROJ_ROWS = 1024
ATT_TQ = 512
ATT_TK = 512
ATT_COLS = 256
ATT_UNROLL = 8
SSM_STEPS = 64
SSM_LANE_GROUPS = 4
FFN_ROWS = 512

_NT = (((1,), (1,)), ((), ()))


def _dot(a, b):
    return jnp.dot(a, b, preferred_element_type=F32)


def _rms(x, g):
    ms = jnp.mean(x * x, axis=-1, keepdims=True)
    return x * lax.rsqrt(ms + EPS) * g


def _inproj_kernel(x_ref, g_ref, wqT_ref, wk_ref, wvT_ref, wu_ref,
                   qT_ref, k_ref, vT_ref, u_ref, *, tq, tk):
    h = _rms(x_ref[0], g_ref[...]).astype(BF16)
    qT = lax.dot_general(wqT_ref[...], h, _NT, preferred_element_type=F32)
    qT = qT.astype(BF16)
    for c in range(qT_ref.shape[1]):
        for hd in range(ATT_HEADS):
            qT_ref[0, c, hd] = qT[hd * 2 * ATT_QK_DIM:(hd + 1) * 2 * ATT_QK_DIM,
                                  c * tq:(c + 1) * tq]
    k_ref[0] = _dot(h, wk_ref[...]).astype(BF16)
    vT = lax.dot_general(wvT_ref[...], h, _NT, preferred_element_type=F32).astype(BF16)
    ones = jnp.ones((ATT_V_ROWS - ATT_V_DIM, tk), BF16)
    for c in range(vT_ref.shape[1]):
        for hd in range(ATT_HEADS):
            r0 = hd * ATT_V_ROWS
            vT_ref[0, c, r0:r0 + ATT_V_DIM, :] = vT[hd * ATT_V_DIM:(hd + 1) * ATT_V_DIM,
                                                    c * tk:(c + 1) * tk]
            vT_ref[0, c, r0 + ATT_V_DIM:r0 + ATT_V_ROWS, :] = ones
    u_ref[0] = _dot(h, wu_ref[...])


def _inproj(x, g, wqT, wk, wvT, wu, *, ts, tq, tk):
    bsz, seq, d = x.shape
    ns = seq // ts
    const = lambda shape: pl.BlockSpec(shape, lambda b, s: (0,) * len(shape),
                                       pipeline_mode=pl.Buffered(1))
    return pl.pallas_call(
        functools.partial(_inproj_kernel, tq=tq, tk=tk),
        grid=(bsz, ns),
        in_specs=[
            pl.BlockSpec((1, ts, d), lambda b, s: (b, s, 0)),
            const((1, d)),
            const((QK_COLS, d)),
            const((d, QK_COLS)),
            const((ATT_WIDTH, d)),
            const((d, SSM_WIDTH)),
        ],
        out_specs=[
            pl.BlockSpec((1, ts // tq, ATT_HEADS, 2 * ATT_QK_DIM, tq),
                         lambda b, s: (b, s, 0, 0, 0)),
            pl.BlockSpec((1, ts, QK_COLS), lambda b, s: (b, s, 0)),
            pl.BlockSpec((1, ts // tk, ATT_HEADS * ATT_V_ROWS, tk), lambda b, s: (b, s, 0, 0)),
            pl.BlockSpec((1, ts, SSM_WIDTH), lambda b, s: (b, s, 0)),
        ],
        out_shape=[
            jax.ShapeDtypeStruct((bsz, seq // tq, ATT_HEADS, 2 * ATT_QK_DIM, tq), BF16),
            jax.ShapeDtypeStruct((bsz, seq, QK_COLS), BF16),
            jax.ShapeDtypeStruct((bsz, seq // tk, ATT_HEADS * ATT_V_ROWS, tk), BF16),
            jax.ShapeDtypeStruct((bsz, seq, SSM_WIDTH), F32),
        ],
        compiler_params=pltpu.CompilerParams(
            dimension_semantics=("arbitrary", "arbitrary"),
            vmem_limit_bytes=V7X_VMEM_LIMIT_BYTES),
        name="inproj",
    )(x, g, wqT, wk, wvT, wu)


def _attn_kernel(lam_ref, qT_ref, k_ref, vT_ref, bias_ref, g_ref, o_ref,
                 qbd, s_a, s_b, mx_a, mx_b, m_s, acc_s, *, tq, tk, out_scale):
    assert tq == tk
    nq = qT_ref.shape[1]
    dk = ATT_QK_DIM
    a, b = bufs = ((s_a, mx_a), (s_b, mx_b))
    nblk = 2 * tq // ATT_COLS

    def q_init(qi):
        for c0 in range(0, tq, ATT_COLS):
            qbd[c0 // ATT_COLS, 0:dk, :] = qT_ref[0, qi, 0, 0:dk, c0:c0 + ATT_COLS]
            qbd[(tq + c0) // ATT_COLS, dk:2 * dk, :] = qT_ref[0, qi, 0, dk:2 * dk,
                                                              c0:c0 + ATT_COLS]
        m_s[...] = jnp.full_like(m_s, -jnp.inf)
        acc_s[...] = jnp.zeros_like(acc_s)

    def finalize(qi):
        lam = lam_ref[0, 0]
        acc = jnp.concatenate([acc_s[cb, 0:ATT_V_DIM, :] for cb in range(nblk)], axis=1)
        l = jnp.concatenate([acc_s[cb, ATT_V_DIM:ATT_V_DIM + 1, :]
                             for cb in range(nblk)], axis=1)
        inv = 1.0 / l
        o = acc[:, 0:tq] * inv[:, 0:tq] - lam * (acc[:, tq:2 * tq] * inv[:, tq:2 * tq])
        ms = jnp.mean(o * o, axis=0, keepdims=True)
        o = o * lax.rsqrt(ms + EPS) * g_ref[...] * out_scale
        o_ref[0, pl.ds(pl.multiple_of(qi * tq, tq), tq), :] = o.T.astype(BF16)

    def live_rows(r, c0):
        if r is None or r == 0:
            return tk
        return max(0, min(tk, c0 % tq + ATT_COLS - (r - 1) * tk))

    def logits(tile, bufs2):
        j, r = tile
        buf, mx = bufs2
        r0 = pl.multiple_of(j * tk, tk)
        for c0 in range(0, 2 * tq, ATT_COLS):
            rows = live_rows(r, c0)
            if rows:
                s = _dot(k_ref[0, pl.ds(r0, rows), :], qbd[c0 // ATT_COLS])
                if r is not None:
                    s = s + bias_ref[0, r, (c0 % tq) // ATT_COLS, 0:rows, :]
                buf[c0 // ATT_COLS, 0:rows, :] = s
                mx[:, c0:c0 + ATT_COLS] = jnp.max(s, axis=0, keepdims=True)

    def softmax_pv(tile, bufs2):
        j, r = tile
        buf, mx = bufs2
        for c0 in range(0, 2 * tq, ATT_COLS):
            rows = live_rows(r, c0)
            if rows == 0:
                continue
            cols = slice(c0, c0 + ATT_COLS)
            m_old = m_s[:, cols]
            m_new = jnp.maximum(m_old, mx[:, cols])
            alpha = jnp.exp2(m_old - m_new)
            p = jnp.exp2(buf[c0 // ATT_COLS, 0:rows, :] - m_new)
            cb = c0 // ATT_COLS
            acc_s[cb] = alpha * acc_s[cb] + _dot(vT_ref[0, j, :, 0:rows], p.astype(BF16))
            m_s[:, cols] = m_new

    def run_tiles(tiles, preloaded):
        if not preloaded:
            logits(tiles[0], bufs[0])
        for i, tile in enumerate(tiles):
            if i + 1 < len(tiles):
                logits(tiles[i + 1], bufs[(i + 1) % 2])
            softmax_pv(tile, bufs[i % 2])

    qbd[...] = jnp.zeros_like(qbd)
    q_init(0)
    run_tiles([(0, 1)], preloaded=False)
    finalize(0)
    q_init(1)
    run_tiles([(0, 0), (1, 1)], preloaded=False)

    def q_block(qi, carry):
        finalize(qi - 1)
        q_init(qi)
        logits((0, None), a)

        un = ATT_UNROLL

        def trip(g, c):
            for i in range(un):
                logits((un * g + i + 1, None), bufs[(i + 1) % 2])
                softmax_pv((un * g + i, None), bufs[i % 2])
            return c

        ntrips = (qi - 2) // un
        lax.fori_loop(0, ntrips, trip, 0)
        t = un * ntrips

        for rem in range(2, un + 2):
            @pl.when(qi - t == rem)
            def _(rem=rem):
                far = [(t + i, None) for i in range(rem - 1)]
                run_tiles(far + [(t + rem - 1, 0), (qi, 1)], preloaded=True)

        return carry

    lax.fori_loop(2, nq, q_block, 0)
    finalize(nq - 1)


def _attention(lam, qT, k, vT, bias, g, *, tq, tk, out_scale):
    bsz, seq, _ = k.shape
    nq = seq // tq
    assert tq == tk and nq >= 2
    return pl.pallas_call(
        functools.partial(_attn_kernel, tq=tq, tk=tk, out_scale=out_scale),
        grid=(bsz, ATT_HEADS),
        in_specs=[
            pl.BlockSpec(memory_space=pltpu.SMEM),
            pl.BlockSpec((1, nq, 1, 2 * ATT_QK_DIM, tq), lambda b, h: (b, 0, h, 0, 0)),
            pl.BlockSpec((1, seq, 2 * ATT_QK_DIM), lambda b, h: (b, 0, h)),
            pl.BlockSpec((1, seq // tk, ATT_V_ROWS, tk), lambda b, h: (b, 0, h, 0)),
            pl.BlockSpec((1, 2, tq // ATT_COLS, tk, ATT_COLS), lambda b, h: (h, 0, 0, 0, 0)),
            pl.BlockSpec((ATT_V_DIM, 1), lambda b, h: (0, 0)),
        ],
        out_specs=pl.BlockSpec((1, seq, ATT_V_DIM), lambda b, h: (b, 0, h)),
        out_shape=jax.ShapeDtypeStruct((bsz, seq, ATT_WIDTH), BF16),
        scratch_shapes=[
            pltpu.VMEM((2 * tq // ATT_COLS, 2 * ATT_QK_DIM, ATT_COLS), BF16),
            pltpu.VMEM((2 * tq // ATT_COLS, tk, ATT_COLS), F32),
            pltpu.VMEM((2 * tq // ATT_COLS, tk, ATT_COLS), F32),
            pltpu.VMEM((1, 2 * tq), F32),
            pltpu.VMEM((1, 2 * tq), F32),
            pltpu.VMEM((1, 2 * tq), F32),
            pltpu.VMEM((2 * tq // ATT_COLS, ATT_V_ROWS, ATT_COLS), F32),
        ],
        compiler_params=pltpu.CompilerParams(
            dimension_semantics=("arbitrary", "arbitrary"),
            vmem_limit_bytes=V7X_VMEM_LIMIT_BYTES),
        name="diffattn",
    )(lam, qT, k, vT, bias, g)


def _ssm_kernel(u_ref, wb_ref, wc_ref, lam_ref, d_ref, wglu_ref, bglu_ref, g_ref,
                o_ref, uin, res, xs, st, *, steps):
    rows = V7X_SUBLANES
    half = xs.shape[2] // 2
    lanes = V7X_LANES
    bsz = u_ref.shape[0]

    @pl.when(pl.program_id(0) == 0)
    def _():
        st[...] = jnp.zeros_like(st)

    for b in range(bsz):
        for c in range(SSM_LANE_GROUPS):
            uin[c, pl.ds(b, steps, stride=bsz), :] = u_ref[b, :, c * lanes:(c + 1) * lanes]

    ys = []
    xs[0] = _dot(uin[0].astype(BF16), wb_ref[0])
    for c in range(SSM_LANE_GROUPS):
        if c + 1 < SSM_LANE_GROUPS:
            xs[c + 1] = _dot(uin[c + 1].astype(BF16), wb_ref[c + 1])
        lre = lam_ref[c, :, 0:half]
        lim = lam_ref[c, :, half:2 * half]

        def step(t, carry, c=c, lre=lre, lim=lim):
            sre, sim = carry
            r = pl.multiple_of(t * rows, rows)
            bre = xs[c, pl.ds(r, rows), 0:half]
            bim = xs[c, pl.ds(r, rows), half:2 * half]
            nre = lre * sre - lim * sim + bre
            nim = lre * sim + lim * sre + bim
            xs[c, pl.ds(r, rows), 0:half] = nre
            xs[c, pl.ds(r, rows), half:2 * half] = nim
            return nre, nim

        sre, sim = lax.fori_loop(0, steps, step,
                                 (st[c, :, 0:half], st[c, :, half:2 * half]), unroll=True)
        st[c, :, 0:half] = sre
        st[c, :, half:2 * half] = sim
        ys.append(_dot(xs[c].astype(BF16), wc_ref[c]))

    u = jnp.concatenate([uin[c] for c in range(SSM_LANE_GROUPS)], axis=1)
    y = jnp.concatenate(ys, axis=1) + d_ref[...] * u
    gl = jax.nn.gelu(y)
    z = _dot(gl.astype(BF16), wglu_ref[...]) + bglu_ref[...]
    out = _rms(gl * jax.nn.sigmoid(z), g_ref[...])
    for c in range(SSM_LANE_GROUPS):
        res[c] = out[:, c * lanes:(c + 1) * lanes]
    for b in range(bsz):
        for c in range(SSM_LANE_GROUPS):
            o_ref[b, :, c * lanes:(c + 1) * lanes] = (
                res[c, pl.ds(b, steps, stride=bsz), :].astype(BF16))


def _ssm(u, wb, wc, lam, dskip, wglu, bglu, g, *, steps):
    bsz, seq, width = u.shape
    r = steps * bsz
    const = lambda shape: pl.BlockSpec(shape, lambda i: (0,) * len(shape),
                                       pipeline_mode=pl.Buffered(1))
    return pl.pallas_call(
        functools.partial(_ssm_kernel, steps=steps),
        grid=(seq // steps,),
        in_specs=[
            pl.BlockSpec((bsz, steps, width), lambda i: (0, i, 0)),
            const(wb.shape), const(wc.shape), const(lam.shape), const(dskip.shape),
            const(wglu.shape), const(bglu.shape), const(g.shape),
        ],
        out_specs=pl.BlockSpec((bsz, steps, width), lambda i: (0, i, 0)),
        out_shape=jax.ShapeDtypeStruct((bsz, seq, width), BF16),
        scratch_shapes=[
            pltpu.VMEM((SSM_LANE_GROUPS, r, V7X_LANES), F32),
            pltpu.VMEM((SSM_LANE_GROUPS, r, V7X_LANES), F32),
            pltpu.VMEM((SSM_LANE_GROUPS, r, wb.shape[2]), F32),
            pltpu.VMEM((SSM_LANE_GROUPS, V7X_SUBLANES, wb.shape[2]), F32),
        ],
        compiler_params=pltpu.CompilerParams(
            dimension_semantics=("arbitrary",),
            vmem_limit_bytes=V7X_VMEM_LIMIT_BYTES),
        name="s5ssm",
    )(u, wb, wc, lam, dskip, wglu, bglu, g)


def _ffn_kernel(x_ref, att_ref, ssm_ref, woa_ref, wos_ref, gf_ref, wg_ref, wu_ref, wd_ref,
                gl_ref, o_ref):
    x1 = x_ref[0] + _dot(att_ref[0], woa_ref[...]) + _dot(ssm_ref[0], wos_ref[...])
    h = _rms(x1, gf_ref[...]).astype(BF16)
    a = (jax.nn.silu(_dot(h, wg_ref[...])) * _dot(h, wu_ref[...])).astype(BF16)
    o_ref[0] = _rms(x1 + _dot(a, wd_ref[...]), gl_ref[...])


def _ffn(x, att, ssm, woa, wos, gf, wg, wu, wd, gl, *, tm):
    bsz, seq, d = x.shape
    const = lambda shape: pl.BlockSpec(shape, lambda b, s: (0,) * len(shape),
                                       pipeline_mode=pl.Buffered(1))
    return pl.pallas_call(
        _ffn_kernel,
        grid=(bsz, seq // tm),
        in_specs=[
            pl.BlockSpec((1, tm, d), lambda b, s: (b, s, 0)),
            pl.BlockSpec((1, tm, ATT_WIDTH), lambda b, s: (b, s, 0)),
            pl.BlockSpec((1, tm, SSM_WIDTH), lambda b, s: (b, s, 0)),
            const(woa.shape), const(wos.shape), const(gf.shape),
            const(wg.shape), const(wu.shape), const(wd.shape), const(gl.shape),
        ],
        out_specs=pl.BlockSpec((1, tm, d), lambda b, s: (b, s, 0)),
        out_shape=jax.ShapeDtypeStruct((bsz, seq, d), x.dtype),
        compiler_params=pltpu.CompilerParams(
            dimension_semantics=("arbitrary", "arbitrary"),
            vmem_limit_bytes=V7X_VMEM_LIMIT_BYTES),
        name="outproj_ffn",
    )(x, att, ssm, woa, wos, gf, wg, wu, wd, gl)


def _t5_bucket(n):
    max_exact = REL_BUCKETS // 2
    nf = jnp.maximum(n, 1).astype(F32)
    large = max_exact + (jnp.log(nf / max_exact) / math.log(REL_MAX_DIST / max_exact)
                         * (REL_BUCKETS - max_exact)).astype(jnp.int32)
    large = jnp.minimum(large, REL_BUCKETS - 1)
    return jnp.where(n < max_exact, n, large)


def _bias_tiles(rel_bias, tq, tk):
    assert tk >= REL_MAX_DIST
    ratio = tq // tk
    nd = tq + tk + 1
    neg = ratio * tk
    table = rel_bias.astype(F32)
    by_dist = table[_t5_bucket(jnp.arange(nd))]
    far = table[_t5_bucket(jnp.full((), REL_MAX_DIST, jnp.int32))]
    by_dist = (by_dist - far[None, :]) * LOG2E
    by_dist = jnp.concatenate([jnp.full((neg, ATT_HEADS), MASK_VALUE, F32), by_dist], axis=0).T
    m = tk + tq
    tiles = []
    for r in range(ratio + 1):
        start = neg - (tk - 1) - (r - 1) * tk
        c = by_dist[:, start:start + m]
        t = jnp.broadcast_to(c[:, None, :], (ATT_HEADS, tk, m)).reshape(ATT_HEADS, tk * m)
        t = t[:, :tk * (m - 1)].reshape(ATT_HEADS, tk, m - 1)
        tiles.append(t[:, :, tk - 1:tk - 1 + tq])
    tiles = jnp.stack(tiles, axis=1)
    tiles = tiles.reshape(ATT_HEADS, ratio + 1, tk, tq // ATT_COLS, ATT_COLS)
    return jnp.transpose(tiles, (0, 1, 3, 2, 4))


def _ssm_params(A_re, A_im, log_dt, B_re, B_im, C_re, C_im):
    lam = lax.complex(A_re.astype(F32), A_im.astype(F32))
    dt = jnp.exp(log_dt.astype(F32))[:, None]
    lam_bar = jnp.exp(lam * dt)
    b_bar = ((lam_bar - 1.0) / lam)[:, :, None] * lax.complex(B_re.astype(F32), B_im.astype(F32))
    nlg = SSM_LANE_GROUPS
    gpl = SSM_GROUPS // nlg
    eye = jnp.eye(gpl, dtype=F32)

    def in_blocks(w):
        w = w.reshape(nlg, gpl, SSM_STATE, SSM_GROUP)
        return jnp.einsum('kgnc,gh->kgchn', w, eye).reshape(nlg, gpl * SSM_GROUP, gpl * SSM_STATE)

    def out_blocks(w):
        w = w.reshape(nlg, gpl, SSM_GROUP, SSM_STATE)
        return jnp.einsum('kgcn,gh->kgnhc', w, eye).reshape(nlg, gpl * SSM_STATE, gpl * SSM_GROUP)

    wb = jnp.concatenate([in_blocks(b_bar.real), in_blocks(b_bar.imag)], axis=2).astype(BF16)
    wc = jnp.concatenate([out_blocks(C_re.astype(F32)), out_blocks(-C_im.astype(F32))],
                         axis=1).astype(BF16)
    lam_ri = jnp.concatenate([lam_bar.real.reshape(nlg, 1, -1), lam_bar.imag.reshape(nlg, 1, -1)],
                             axis=2)
    lam_ri = jnp.broadcast_to(lam_ri, (nlg, V7X_SUBLANES, lam_ri.shape[2]))
    return wb, wc, lam_ri


def kernel(x, norm_mix_g, w_in, lambda_q1, lambda_k1, lambda_q2, lambda_k2, subln_g, rel_bias,
           A_re, A_im, log_dt, B_re, B_im, C_re, C_im, D_skip, w_glu, b_glu, ssm_norm_g, w_out,
           norm_ffn_g, w_gate, w_up, w_down, norm_final_g):
    bsz, seq, d = x.shape
    assert d == D_MODEL and bsz == V7X_SUBLANES
    ts, tq, tk, tm = PROJ_ROWS, ATT_TQ, ATT_TK, FFN_ROWS
    assert seq % ts == 0 and seq % tq == 0 and seq % tm == 0 and seq % SSM_STEPS == 0
    assert ts % tk == 0 and ts % tq == 0 and tq == tk
    l = 0
    row = lambda v: v.astype(F32).reshape(1, -1)

    w = w_in[l].astype(F32)
    qk_scale = ATT_QK_DIM ** -0.5 * LOG2E
    wqT = (w[:, 0:QK_COLS] * qk_scale).T.astype(BF16)
    wk = w[:, QK_COLS:2 * QK_COLS].astype(BF16)
    wvT = w[:, 2 * QK_COLS:2 * QK_COLS + ATT_WIDTH].T.astype(BF16)
    wu = w[:, 2 * QK_COLS + ATT_WIDTH:].astype(BF16)
    lam_init = 0.8 - 0.6 * math.exp(-0.3 * l)
    lam = (jnp.exp(jnp.sum(lambda_q1[l].astype(F32) * lambda_k1[l].astype(F32)))
           - jnp.exp(jnp.sum(lambda_q2[l].astype(F32) * lambda_k2[l].astype(F32)))
           + lam_init).reshape(1, 1)
    bias = _bias_tiles(rel_bias, tq, tk)
    wb, wc, lam_ri = _ssm_params(A_re[l], A_im[l], log_dt[l], B_re[l], B_im[l], C_re[l], C_im[l])

    qT, k, vT, u = _inproj(x, row(norm_mix_g[l]), wqT, wk, wvT, wu, ts=ts, tq=tq, tk=tk)
    att = _attention(lam, qT, k, vT, bias, subln_g[l].astype(F32).reshape(-1, 1),
                     tq=tq, tk=tk, out_scale=1.0 - lam_init)
    ssm = _ssm(u, wb, wc, lam_ri, row(D_skip[l]),
               w_glu[l].astype(BF16), row(b_glu[l]), row(ssm_norm_g[l]), steps=SSM_STEPS)

    wo = w_out[l].astype(BF16)
    return _ffn(x, att, ssm, wo[0:ATT_WIDTH], wo[ATT_WIDTH:],
                row(norm_ffn_g[l]), w_gate[l].astype(BF16), w_up[l].astype(BF16),
                w_down[l].astype(BF16), row(norm_final_g), tm=tm)
```

```python
import functools
import math

import jax
import jax.numpy as jnp
from jax import lax
from jax.experimental import pallas as pl
from jax.experimental.pallas import tpu as pltpu

F32 = jnp.float32
BF16 = jnp.bfloat16

D_MODEL = 1024
ATT_WIDTH = 512
SSM_WIDTH = 512
ATT_HEADS = 4
ATT_QK_DIM = 64
ATT_V_DIM = 128
QK_COLS = 512
SSM_GROUP = 16
SSM_GROUPS = 32
SSM_STATE = 64
REL_BUCKETS = 32
REL_MAX_DIST = 128
EPS = 1e-6
LOG2E = math.log2(math.e)
MASK_VALUE = -1e30

V7X_SUBLANES = 8
V7X_LANES = 128
V7X_VMEM_LIMIT_BYTES = 56 * 1024 * 1024

PROJ_ROWS = 1024
ATT_TQ = 512
ATT_TK = 512
ATT_COLS = 256
ATT_UNROLL = 8
SSM_STEPS = 64
SSM_LANE_GROUPS = 4
FFN_ROWS = 512

_NT = (((1,), (1,)), ((), ()))


def _dot(a, b):
    return jnp.dot(a, b, preferred_element_type=F32)


def _rms(x, g):
    ms = jnp.mean(x * x, axis=-1, keepdims=True)
    return x * lax.rsqrt(ms + EPS) * g


def _inproj_kernel(x_ref, g_ref, wqT_ref, wk_ref, wvT_ref, wu_ref,
                   qT_ref, k_ref, vT_ref, u_ref, *, tq, tk):
    h = _rms(x_ref[0], g_ref[...]).astype(BF16)
    qT = lax.dot_general(wqT_ref[...], h, _NT, preferred_element_type=F32)
    qT = qT.astype(BF16)
    for c in range(qT_ref.shape[1]):
        for hd in range(ATT_HEADS):
            qT_ref[0, c, hd] = qT[hd * 2 * ATT_QK_DIM:(hd + 1) * 2 * ATT_QK_DIM,
                                  c * tq:(c + 1) * tq]
    k_ref[0] = _dot(h, wk_ref[...]).astype(BF16)
    vT = lax.dot_general(wvT_ref[...], h, _NT, preferred_element_type=F32).astype(BF16)
    for c in range(vT_ref.shape[1]):
        vT_ref[0, c] = vT[:, c * tk:(c + 1) * tk]
    u_ref[0] = _dot(h, wu_ref[...])


def _inproj(x, g, wqT, wk, wvT, wu, *, ts, tq, tk):
    bsz, seq, d = x.shape
    ns = seq // ts
    const = lambda shape: pl.BlockSpec(shape, lambda b, s: (0,) * len(shape),
                                       pipeline_mode=pl.Buffered(1))
    return pl.pallas_call(
        functools.partial(_inproj_kernel, tq=tq, tk=tk),
        grid=(bsz, ns),
        in_specs=[
            pl.BlockSpec((1, ts, d), lambda b, s: (b, s, 0)),
            const((1, d)),
            const((QK_COLS, d)),
            const((d, QK_COLS)),
            const((ATT_WIDTH, d)),
            const((d, SSM_WIDTH)),
        ],
        out_specs=[
            pl.BlockSpec((1, ts // tq, ATT_HEADS, 2 * ATT_QK_DIM, tq),
                         lambda b, s: (b, s, 0, 0, 0)),
            pl.BlockSpec((1, ts, QK_COLS), lambda b, s: (b, s, 0)),
            pl.BlockSpec((1, ts // tk, ATT_WIDTH, tk), lambda b, s: (b, s, 0, 0)),
            pl.BlockSpec((1, ts, SSM_WIDTH), lambda b, s: (b, s, 0)),
        ],
        out_shape=[
            jax.ShapeDtypeStruct((bsz, seq // tq, ATT_HEADS, 2 * ATT_QK_DIM, tq), BF16),
            jax.ShapeDtypeStruct((bsz, seq, QK_COLS), BF16),
            jax.ShapeDtypeStruct((bsz, seq // tk, ATT_WIDTH, tk), BF16),
            jax.ShapeDtypeStruct((bsz, seq, SSM_WIDTH), F32),
        ],
        compiler_params=pltpu.CompilerParams(
            dimension_semantics=("arbitrary", "arbitrary"),
            vmem_limit_bytes=V7X_VMEM_LIMIT_BYTES),
        name="inproj",
    )(x, g, wqT, wk, wvT, wu)


def _attn_kernel(lam_ref, qT_ref, k_ref, vT_ref, bias_ref, g_ref, o_ref,
                 qbd, s_a, s_b, mx_a, mx_b, m_s, l_s, acc_s, *, tq, tk, out_scale):
    assert tq == tk
    nq = qT_ref.shape[1]
    dk = ATT_QK_DIM
    a, b = bufs = ((s_a, mx_a), (s_b, mx_b))
    nblk = 2 * tq // ATT_COLS

    def q_init(qi):
        for c0 in range(0, tq, ATT_COLS):
            qbd[c0 // ATT_COLS, 0:dk, :] = qT_ref[0, qi, 0, 0:dk, c0:c0 + ATT_COLS]
            qbd[(tq + c0) // ATT_COLS, dk:2 * dk, :] = qT_ref[0, qi, 0, dk:2 * dk,
                                                              c0:c0 + ATT_COLS]
        m_s[...] = jnp.full_like(m_s, -jnp.inf)
        l_s[...] = jnp.zeros_like(l_s)
        acc_s[...] = jnp.zeros_like(acc_s)

    def finalize(qi):
        lam = lam_ref[0, 0]
        acc = jnp.concatenate([acc_s[cb] for cb in range(nblk)], axis=1)
        l = l_s[...]
        inv = 1.0 / l
        o = acc[:, 0:tq] * inv[:, 0:tq] - lam * (acc[:, tq:2 * tq] * inv[:, tq:2 * tq])
        ms = jnp.mean(o * o, axis=0, keepdims=True)
        o = o * lax.rsqrt(ms + EPS) * g_ref[...] * out_scale
        o_ref[0, pl.ds(pl.multiple_of(qi * tq, tq), tq), :] = o.T.astype(BF16)

    def live_rows(r, c0):
        if r is None or r == 0:
            return tk
        return max(0, min(tk, c0 % tq + ATT_COLS - (r - 1) * tk))

    def logits(tile, bufs2):
        j, r = tile
        buf, mx = bufs2
        r0 = pl.multiple_of(j * tk, tk)
        for c0 in range(0, 2 * tq, ATT_COLS):
            rows = live_rows(r, c0)
            if rows:
                s = _dot(k_ref[0, pl.ds(r0, rows), :], qbd[c0 // ATT_COLS])
                if r is not None:
                    s = s + bias_ref[0, r, (c0 % tq) // ATT_COLS, 0:rows, :]
                buf[c0 // ATT_COLS, 0:rows, :] = s
                mx[:, c0:c0 + ATT_COLS] = jnp.max(s, axis=0, keepdims=True)

    def softmax_pv(tile, bufs2):
        j, r = tile
        buf, mx = bufs2
        for c0 in range(0, 2 * tq, ATT_COLS):
            rows = live_rows(r, c0)
            if rows == 0:
                continue
            cols = slice(c0, c0 + ATT_COLS)
            m_old = m_s[:, cols]
            m_new = jnp.maximum(m_old, mx[:, cols])
            alpha = jnp.exp2(m_old - m_new)
            p = jnp.exp2(buf[c0 // ATT_COLS, 0:rows, :] - m_new)
            cb = c0 // ATT_COLS
            l_s[:, cols] = alpha * l_s[:, cols] + jnp.sum(p, axis=0, keepdims=True)
            acc_s[cb] = alpha * acc_s[cb] + _dot(vT_ref[0, j, :, 0:rows], p.astype(BF16))
            m_s[:, cols] = m_new

    def run_tiles(tiles, preloaded):
        if not preloaded:
            logits(tiles[0], bufs[0])
        for i, tile in enumerate(tiles):
            if i + 1 < len(tiles):
                logits(tiles[i + 1], bufs[(i + 1) % 2])
            softmax_pv(tile, bufs[i % 2])

    qbd[...] = jnp.zeros_like(qbd)
    q_init(0)
    run_tiles([(0, 1)], preloaded=False)
    finalize(0)
    q_init(1)
    run_tiles([(0, 0), (1, 1)], preloaded=False)

    def q_block(qi, carry):
        finalize(qi - 1)
        q_init(qi)
        logits((0, None), a)

        un = ATT_UNROLL

        def trip(g, c):
            for i in range(un):
                logits((un * g + i + 1, None), bufs[(i + 1) % 2])
                softmax_pv((un * g + i, None), bufs[i % 2])
            return c

        ntrips = (qi - 2) // un
        lax.fori_loop(0, ntrips, trip, 0)
        t = un * ntrips

        for rem in range(2, un + 2):
            @pl.when(qi - t == rem)
            def _(rem=rem):
                far = [(t + i, None) for i in range(rem - 1)]
                run_tiles(far + [(t + rem - 1, 0), (qi, 1)], preloaded=True)

        return carry

    lax.fori_loop(2, nq, q_block, 0)
    finalize(nq - 1)


def _attention(lam, qT, k, vT, bias, g, *, tq, tk, out_scale):
    bsz, seq, _ = k.shape
    nq = seq // tq
    assert tq == tk and nq >= 2
    return pl.pallas_call(
        functools.partial(_attn_kernel, tq=tq, tk=tk, out_scale=out_scale),
        grid=(bsz, ATT_HEADS),
        in_specs=[
            pl.BlockSpec(memory_space=pltpu.SMEM),
            pl.BlockSpec((1, nq, 1, 2 * ATT_QK_DIM, tq), lambda b, h: (b, 0, h, 0, 0)),
            pl.BlockSpec((1, seq, 2 * ATT_QK_DIM), lambda b, h: (b, 0, h)),
            pl.BlockSpec((1, seq // tk, ATT_V_DIM, tk), lambda b, h: (b, 0, h, 0)),
            pl.BlockSpec((1, 2, tq // ATT_COLS, tk, ATT_COLS), lambda b, h: (h, 0, 0, 0, 0)),
            pl.BlockSpec((ATT_V_DIM, 1), lambda b, h: (0, 0)),
        ],
        out_specs=pl.BlockSpec((1, seq, ATT_V_DIM), lambda b, h: (b, 0, h)),
        out_shape=jax.ShapeDtypeStruct((bsz, seq, ATT_WIDTH), BF16),
        scratch_shapes=[
            pltpu.VMEM((2 * tq // ATT_COLS, 2 * ATT_QK_DIM, ATT_COLS), BF16),
            pltpu.VMEM((2 * tq // ATT_COLS, tk, ATT_COLS), F32),
            pltpu.VMEM((2 * tq // ATT_COLS, tk, ATT_COLS), F32),
            pltpu.VMEM((1, 2 * tq), F32),
            pltpu.VMEM((1, 2 * tq), F32),
            pltpu.VMEM((1, 2 * tq), F32),
            pltpu.VMEM((1, 2 * tq), F32),
            pltpu.VMEM((2 * tq // ATT_COLS, ATT_V_DIM, ATT_COLS), F32),
        ],
        compiler_params=pltpu.CompilerParams(
            dimension_semantics=("arbitrary", "arbitrary"),
            vmem_limit_bytes=V7X_VMEM_LIMIT_BYTES),
        name="diffattn",
    )(lam, qT, k, vT, bias, g)


def _ssm_kernel(u_ref, wb_ref, wc_ref, lam_ref, d_ref, wglu_ref, bglu_ref, g_ref,
                o_ref, uin, res, xs, st, *, steps):
    rows = V7X_SUBLANES
    half = xs.shape[2] // 2
    lanes = V7X_LANES
    bsz = u_ref.shape[0]

    @pl.when(pl.program_id(0) == 0)
    def _():
        st[...] = jnp.zeros_like(st)

    for b in range(bsz):
        for c in range(SSM_LANE_GROUPS):
            uin[c, pl.ds(b, steps, stride=bsz), :] = u_ref[b, :, c * lanes:(c + 1) * lanes]

    ys = []
    xs[0] = _dot(uin[0].astype(BF16), wb_ref[0])
    for c in range(SSM_LANE_GROUPS):
        if c + 1 < SSM_LANE_GROUPS:
            xs[c + 1] = _dot(uin[c + 1].astype(BF16), wb_ref[c + 1])
        lre = lam_ref[c, :, 0:half]
        lim = lam_ref[c, :, half:2 * half]

        def step(t, carry, c=c, lre=lre, lim=lim):
            sre, sim = carry
            r = pl.multiple_of(t * rows, rows)
            bre = xs[c, pl.ds(r, rows), 0:half]
            bim = xs[c, pl.ds(r, rows), half:2 * half]
            nre = lre * sre - lim * sim + bre
            nim = lre * sim + lim * sre + bim
            xs[c, pl.ds(r, rows), 0:half] = nre
            xs[c, pl.ds(r, rows), half:2 * half] = nim
            return nre, nim

        sre, sim = lax.fori_loop(0, steps, step,
                                 (st[c, :, 0:half], st[c, :, half:2 * half]), unroll=True)
        st[c, :, 0:half] = sre
        st[c, :, half:2 * half] = sim
        ys.append(_dot(xs[c].astype(BF16), wc_ref[c]))

    u = jnp.concatenate([uin[c] for c in range(SSM_LANE_GROUPS)], axis=1)
    y = jnp.concatenate(ys, axis=1) + d_ref[...] * u
    gl = jax.nn.gelu(y)
    z = _dot(gl.astype(BF16), wglu_ref[...]) + bglu_ref[...]
    out = _rms(gl * jax.nn.sigmoid(z), g_ref[...])
    for c in range(SSM_LANE_GROUPS):
        res[c] = out[:, c * lanes:(c + 1) * lanes]
    for b in range(bsz):
        for c in range(SSM_LANE_GROUPS):
            o_ref[b, :, c * lanes:(c + 1) * lanes] = (
                res[c, pl.ds(b, steps, stride=bsz), :].astype(BF16))


def _ssm(u, wb, wc, lam, dskip, wglu, bglu, g, *, steps):
    bsz, seq, width = u.shape
    r = steps * bsz
    const = lambda shape: pl.BlockSpec(shape, lambda i: (0,) * len(shape),
                                       pipeline_mode=pl.Buffered(1))
    return pl.pallas_call(
        functools.partial(_ssm_kernel, steps=steps),
        grid=(seq // steps,),
        in_specs=[
            pl.BlockSpec((bsz, steps, width), lambda i: (0, i, 0)),
            const(wb.shape), const(wc.shape), const(lam.shape), const(dskip.shape),
            const(wglu.shape), const(bglu.shape), const(g.shape),
        ],
        out_specs=pl.BlockSpec((bsz, steps, width), lambda i: (0, i, 0)),
        out_shape=jax.ShapeDtypeStruct((bsz, seq, width), BF16),
        scratch_shapes=[
            pltpu.VMEM((SSM_LANE_GROUPS, r, V7X_LANES), F32),
            pltpu.VMEM((SSM_LANE_GROUPS, r, V7X_LANES), F32),
            pltpu.VMEM((SSM_LANE_GROUPS, r, wb.shape[2]), F32),
            pltpu.VMEM((SSM_LANE_GROUPS, V7X_SUBLANES, wb.shape[2]), F32),
        ],
        compiler_params=pltpu.CompilerParams(
            dimension_semantics=("arbitrary",),
            vmem_limit_bytes=V7X_VMEM_LIMIT_BYTES),
        name="s5ssm",
    )(u, wb, wc, lam, dskip, wglu, bglu, g)


def _ffn_kernel(x_ref, att_ref, ssm_ref, woa_ref, wos_ref, gf_ref, wg_ref, wu_ref, wd_ref,
                gl_ref, o_ref):
    x1 = x_ref[0] + _dot(att_ref[0], woa_ref[...]) + _dot(ssm_ref[0], wos_ref[...])
    h = _rms(x1, gf_ref[...]).astype(BF16)
    a = (jax.nn.silu(_dot(h, wg_ref[...])) * _dot(h, wu_ref[...])).astype(BF16)
    o_ref[0] = _rms(x1 + _dot(a, wd_ref[...]), gl_ref[...])


def _ffn(x, att, ssm, woa, wos, gf, wg, wu, wd, gl, *, tm):
    bsz, seq, d = x.shape
    const = lambda shape: pl.BlockSpec(shape, lambda b, s: (0,) * len(shape),
                                       pipeline_mode=pl.Buffered(1))
    return pl.pallas_call(
        _ffn_kernel,
        grid=(bsz, seq // tm),
        in_specs=[
            pl.BlockSpec((1, tm, d), lambda b, s: (b, s, 0)),
            pl.BlockSpec((1, tm, ATT_WIDTH), lambda b, s: (b, s, 0)),
            pl.BlockSpec((1, tm, SSM_WIDTH), lambda b, s: (b, s, 0)),
            const(woa.shape), const(wos.shape), const(gf.shape),
            const(wg.shape), const(wu.shape), const(wd.shape), const(gl.shape),
        ],
        out_specs=pl.BlockSpec((1, tm, d), lambda b, s: (b, s, 0)),
        out_shape=jax.ShapeDtypeStruct((bsz, seq, d), x.dtype),
        compiler_params=pltpu.CompilerParams(
            dimension_semantics=("arbitrary", "arbitrary"),
            vmem_limit_bytes=V7X_VMEM_LIMIT_BYTES),
        name="outproj_ffn",
    )(x, att, ssm, woa, wos, gf, wg, wu, wd, gl)


def _t5_bucket(n):
    max_exact = REL_BUCKETS // 2
    nf = jnp.maximum(n, 1).astype(F32)
    large = max_exact + (jnp.log(nf / max_exact) / math.log(REL_MAX_DIST / max_exact)
                         * (REL_BUCKETS - max_exact)).astype(jnp.int32)
    large = jnp.minimum(large, REL_BUCKETS - 1)
    return jnp.where(n < max_exact, n, large)


def _bias_tiles(rel_bias, tq, tk):
    assert tk >= REL_MAX_DIST
    ratio = tq // tk
    nd = tq + tk + 1
    neg = ratio * tk
    table = rel_bias.astype(F32)
    by_dist = table[_t5_bucket(jnp.arange(nd))]
    far = table[_t5_bucket(jnp.full((), REL_MAX_DIST, jnp.int32))]
    by_dist = (by_dist - far[None, :]) * LOG2E
    by_dist = jnp.concatenate([jnp.full((neg, ATT_HEADS), MASK_VALUE, F32), by_dist], axis=0).T
    m = tk + tq
    tiles = []
    for r in range(ratio + 1):
        start = neg - (tk - 1) - (r - 1) * tk
        c = by_dist[:, start:start + m]
        t = jnp.broadcast_to(c[:, None, :], (ATT_HEADS, tk, m)).reshape(ATT_HEADS, tk * m)
        t = t[:, :tk * (m - 1)].reshape(ATT_HEADS, tk, m - 1)
        tiles.append(t[:, :, tk - 1:tk - 1 + tq])
    tiles = jnp.stack(tiles, axis=1)
    tiles = tiles.reshape(ATT_HEADS, ratio + 1, tk, tq // ATT_COLS, ATT_COLS)
    return jnp.transpose(tiles, (0, 1, 3, 2, 4))


def _ssm_params(A_re, A_im, log_dt, B_re, B_im, C_re, C_im):
    lam = lax.complex(A_re.astype(F32), A_im.astype(F32))
    dt = jnp.exp(log_dt.astype(F32))[:, None]
    lam_bar = jnp.exp(lam * dt)
    b_bar = ((lam_bar - 1.0) / lam)[:, :, None] * lax.complex(B_re.astype(F32), B_im.astype(F32))
    nlg = SSM_LANE_GROUPS
    gpl = SSM_GROUPS // nlg
    eye = jnp.eye(gpl, dtype=F32)

    def in_blocks(w):
        w = w.reshape(nlg, gpl, SSM_STATE, SSM_GROUP)
        return jnp.einsum('kgnc,gh->kgchn', w, eye).reshape(nlg, gpl * SSM_GROUP, gpl * SSM_STATE)

    def out_blocks(w):
        w = w.reshape(nlg, gpl, SSM_GROUP, SSM_STATE)
        return jnp.einsum('kgcn,gh->kgnhc', w, eye).reshape(nlg, gpl * SSM_STATE, gpl * SSM_GROUP)

    wb = jnp.concatenate([in_blocks(b_bar.real), in_blocks(b_bar.imag)], axis=2).astype(BF16)
    wc = jnp.concatenate([out_blocks(C_re.astype(F32)), out_blocks(-C_im.astype(F32))],
                         axis=1).astype(BF16)
    lam_ri = jnp.concatenate([lam_bar.real.reshape(nlg, 1, -1), lam_bar.imag.reshape(nlg, 1, -1)],
                             axis=2)
    lam_ri = jnp.broadcast_to(lam_ri, (nlg, V7X_SUBLANES, lam_ri.shape[2]))
    return wb, wc, lam_ri


def kernel(x, norm_mix_g, w_in, lambda_q1, lambda_k1, lambda_q2, lambda_k2, subln_g, rel_bias,
           A_re, A_im, log_dt, B_re, B_im, C_re, C_im, D_skip, w_glu, b_glu, ssm_norm_g, w_out,
           norm_ffn_g, w_gate, w_up, w_down, norm_final_g):
    bsz, seq, d = x.shape
    assert d == D_MODEL and bsz == V7X_SUBLANES
    ts, tq, tk, tm = PROJ_ROWS, ATT_TQ, ATT_TK, FFN_ROWS
    assert seq % ts == 0 and seq % tq == 0 and seq % tm == 0 and seq % SSM_STEPS == 0
    assert ts % tk == 0 and ts % tq == 0 and tq == tk
    l = 0
    row = lambda v: v.astype(F32).reshape(1, -1)

    w = w_in[l].astype(F32)
    qk_scale = ATT_QK_DIM ** -0.5 * LOG2E
    wqT = (w[:, 0:QK_COLS] * qk_scale).T.astype(BF16)
    wk = w[:, QK_COLS:2 * QK_COLS].astype(BF16)
    wvT = w[:, 2 * QK_COLS:2 * QK_COLS + ATT_WIDTH].T.astype(BF16)
    wu = w[:, 2 * QK_COLS + ATT_WIDTH:].astype(BF16)
    lam_init = 0.8 - 0.6 * math.exp(-0.3 * l)
    lam = (jnp.exp(jnp.sum(lambda_q1[l].astype(F32) * lambda_k1[l].astype(F32)))
           - jnp.exp(jnp.sum(lambda_q2[l].astype(F32) * lambda_k2[l].astype(F32)))
           + lam_init).reshape(1, 1)
    bias = _bias_tiles(rel_bias, tq, tk)
    wb, wc, lam_ri = _ssm_params(A_re[l], A_im[l], log_dt[l], B_re[l], B_im[l], C_re[l], C_im[l])

    qT, k, vT, u = _inproj(x, row(norm_mix_g[l]), wqT, wk, wvT, wu, ts=ts, tq=tq, tk=tk)
    att = _attention(lam, qT, k, vT, bias, subln_g[l].astype(F32).reshape(-1, 1),
                     tq=tq, tk=tk, out_scale=1.0 - lam_init)
    ssm = _ssm(u, wb, wc, lam_ri, row(D_skip[l]),
               w_glu[l].astype(BF16), row(b_glu[l]), row(ssm_norm_g[l]), steps=SSM_STEPS)

    wo = w_out[l].astype(BF16)
    return _ffn(x, att, ssm, wo[0:ATT_WIDTH], wo[ATT_WIDTH:],
                row(norm_ffn_g[l]), w_gate[l].astype(BF16), w_up[l].astype(BF16),
                w_down[l].astype(BF16), row(norm_final_g), tm=tm)
```

```python
import functools
import math

import jax
import jax.numpy as jnp
from jax import lax
from jax.experimental import pallas as pl
from jax.experimental.pallas import tpu as pltpu

F32 = jnp.float32
BF16 = jnp.bfloat16

D_MODEL = 1024
ATT_WIDTH = 512
SSM_WIDTH = 512
ATT_HEADS = 4
ATT_QK_DIM = 64
ATT_V_DIM = 128
ATT_V_ROWS = ATT_V_DIM + 16
QK_COLS = 512
SSM_GROUP = 16
SSM_GROUPS = 32
SSM_STATE = 64
REL_BUCKETS = 32
REL_MAX_DIST = 128
EPS = 1e-6
LOG2E = math.log2(math.e)
MASK_VALUE = -1e30

V7X_SUBLANES = 8
V7X_LANES = 128
V7X_VMEM_LIMIT_BYTES = 56 * 1024 * 1024

PROJ_ROWS = 1024
ATT_TQ = 512
ATT_TK = 512
ATT_COLS = 256
ATT_UNROLL = 8
SSM_STEPS = 128
SSM_LANE_GROUPS = 4
FFN_ROWS = 512

_NT = (((1,), (1,)), ((), ()))


def _dot(a, b):
    return jnp.dot(a, b, preferred_element_type=F32)


def _rms(x, g):
    ms = jnp.mean(x * x, axis=-1, keepdims=True)
    return x * lax.rsqrt(ms + EPS) * g


def _inproj_kernel(x_ref, g_ref, wqT_ref, wk_ref, wvT_ref, wu_ref,
                   qT_ref, k_ref, vT_ref, u_ref, *, tq, tk):
    h = _rms(x_ref[0], g_ref[...]).astype(BF16)
    qT = lax.dot_general(wqT_ref[...], h, _NT, preferred_element_type=F32)
    qT = qT.astype(BF16)
    for c in range(qT_ref.shape[1]):
        for hd in range(ATT_HEADS):
            qT_ref[0, c, hd] = qT[hd * 2 * ATT_QK_DIM:(hd + 1) * 2 * ATT_QK_DIM,
                                  c * tq:(c + 1) * tq]
    k_ref[0] = _dot(h, wk_ref[...]).astype(BF16)
    vT = lax.dot_general(wvT_ref[...], h, _NT, preferred_element_type=F32).astype(BF16)
    ones = jnp.ones((ATT_V_ROWS - ATT_V_DIM, tk), BF16)
    for c in range(vT_ref.shape[1]):
        for hd in range(ATT_HEADS):
            r0 = hd * ATT_V_ROWS
            vT_ref[0, c, r0:r0 + ATT_V_DIM, :] = vT[hd * ATT_V_DIM:(hd + 1) * ATT_V_DIM,
                                                    c * tk:(c + 1) * tk]
            vT_ref[0, c, r0 + ATT_V_DIM:r0 + ATT_V_ROWS, :] = ones
    u_ref[0] = _dot(h, wu_ref[...])


def _inproj(x, g, wqT, wk, wvT, wu, *, ts, tq, tk):
    bsz, seq, d = x.shape
    ns = seq // ts
    const = lambda shape: pl.BlockSpec(shape, lambda b, s: (0,) * len(shape),
                                       pipeline_mode=pl.Buffered(1))
    return pl.pallas_call(
        functools.partial(_inproj_kernel, tq=tq, tk=tk),
        grid=(bsz, ns),
        in_specs=[
            pl.BlockSpec((1, ts, d), lambda b, s: (b, s, 0)),
            const((1, d)),
            const((QK_COLS, d)),
            const((d, QK_COLS)),
            const((ATT_WIDTH, d)),
            const((d, SSM_WIDTH)),
        ],
        out_specs=[
            pl.BlockSpec((1, ts // tq, ATT_HEADS, 2 * ATT_QK_DIM, tq),
                         lambda b, s: (b, s, 0, 0, 0)),
            pl.BlockSpec((1, ts, QK_COLS), lambda b, s: (b, s, 0)),
            pl.BlockSpec((1, ts // tk, ATT_HEADS * ATT_V_ROWS, tk), lambda b, s: (b, s, 0, 0)),
            pl.BlockSpec((1, ts, SSM_WIDTH), lambda b, s: (b, s, 0)),
        ],
        out_shape=[
            jax.ShapeDtypeStruct((bsz, seq // tq, ATT_HEADS, 2 * ATT_QK_DIM, tq), BF16),
            jax.ShapeDtypeStruct((bsz, seq, QK_COLS), BF16),
            jax.ShapeDtypeStruct((bsz, seq // tk, ATT_HEADS * ATT_V_ROWS, tk), BF16),
            jax.ShapeDtypeStruct((bsz, seq, SSM_WIDTH), F32),
        ],
        compiler_params=pltpu.CompilerParams(
            dimension_semantics=("arbitrary", "arbitrary"),
            vmem_limit_bytes=V7X_VMEM_LIMIT_BYTES),
        name="inproj",
    )(x, g, wqT, wk, wvT, wu)


def _attn_kernel(lam_ref, qT_ref, k_ref, vT_ref, bias_ref, g_ref, o_ref,
                 qbd, s_a, s_b, mx_a, mx_b, m_s, acc_s, *, tq, tk, out_scale):
    assert tq == tk
    nq = qT_ref.shape[1]
    dk = ATT_QK_DIM
    a, b = bufs = ((s_a, mx_a), (s_b, mx_b))
    nblk = 2 * tq // ATT_COLS

    def q_init(qi):
        for c0 in range(0, tq, ATT_COLS):
            qbd[c0 // ATT_COLS, 0:dk, :] = qT_ref[0, qi, 0, 0:dk, c0:c0 + ATT_COLS]
            qbd[(tq + c0) // ATT_COLS, dk:2 * dk, :] = qT_ref[0, qi, 0, dk:2 * dk,
                                                              c0:c0 + ATT_COLS]
        m_s[...] = jnp.full_like(m_s, -jnp.inf)
        acc_s[...] = jnp.zeros_like(acc_s)

    def finalize(qi):
        lam = lam_ref[0, 0]
        acc = jnp.concatenate([acc_s[cb, 0:ATT_V_DIM, :] for cb in range(nblk)], axis=1)
        l = jnp.concatenate([acc_s[cb, ATT_V_DIM:ATT_V_DIM + 1, :]
                             for cb in range(nblk)], axis=1)
        inv = 1.0 / l
        o = acc[:, 0:tq] * inv[:, 0:tq] - lam * (acc[:, tq:2 * tq] * inv[:, tq:2 * tq])
        ms = jnp.mean(o * o, axis=0, keepdims=True)
        o = o * lax.rsqrt(ms + EPS) * g_ref[...] * out_scale
        o_ref[0, pl.ds(pl.multiple_of(qi * tq, tq), tq), :] = o.T.astype(BF16)

    def live_rows(r, c0):
        if r is None or r == 0:
            return tk
        return max(0, min(tk, c0 % tq + ATT_COLS - (r - 1) * tk))

    def logits(tile, bufs2):
        j, r = tile
        buf, mx = bufs2
        r0 = pl.multiple_of(j * tk, tk)
        for c0 in range(0, 2 * tq, ATT_COLS):
            rows = live_rows(r, c0)
            if rows:
                s = _dot(k_ref[0, pl.ds(r0, rows), :], qbd[c0 // ATT_COLS])
                if r is not None:
                    s = s + bias_ref[0, r, (c0 % tq) // ATT_COLS, 0:rows, :]
                buf[c0 // ATT_COLS, 0:rows, :] = s
                mx[:, c0:c0 + ATT_COLS] = jnp.max(s, axis=0, keepdims=True)

    def softmax_pv(tile, bufs2):
        j, r = tile
        buf, mx = bufs2
        for c0 in range(0, 2 * tq, ATT_COLS):
            rows = live_rows(r, c0)
            if rows == 0:
                continue
            cols = slice(c0, c0 + ATT_COLS)
            m_old = m_s[:, cols]
            m_new = jnp.maximum(m_old, mx[:, cols])
            alpha = jnp.exp2(m_old - m_new)
            p = jnp.exp2(buf[c0 // ATT_COLS, 0:rows, :] - m_new)
            cb = c0 // ATT_COLS
            acc_s[cb] = alpha * acc_s[cb] + _dot(vT_ref[0, j, :, 0:rows], p.astype(BF16))
            m_s[:, cols] = m_new

    def run_tiles(tiles, preloaded):
        if not preloaded:
            logits(tiles[0], bufs[0])
        for i, tile in enumerate(tiles):
            if i + 1 < len(tiles):
                logits(tiles[i + 1], bufs[(i + 1) % 2])
            softmax_pv(tile, bufs[i % 2])

    qbd[...] = jnp.zeros_like(qbd)
    q_init(0)
    run_tiles([(0, 1)], preloaded=False)
    finalize(0)
    q_init(1)
    run_tiles([(0, 0), (1, 1)], preloaded=False)

    def q_block(qi, carry):
        finalize(qi - 1)
        q_init(qi)
        logits((0, None), a)

        un = ATT_UNROLL

        def trip(g, c):
            for i in range(un):
                logits((un * g + i + 1, None), bufs[(i + 1) % 2])
                softmax_pv((un * g + i, None), bufs[i % 2])
            return c

        ntrips = (qi - 2) // un
        lax.fori_loop(0, ntrips, trip, 0)
        t = un * ntrips

        for rem in range(2, un + 2):
            @pl.when(qi - t == rem)
            def _(rem=rem):
                far = [(t + i, None) for i in range(rem - 1)]
                run_tiles(far + [(t + rem - 1, 0), (qi, 1)], preloaded=True)

        return carry

    lax.fori_loop(2, nq, q_block, 0)
    finalize(nq - 1)


def _attention(lam, qT, k, vT, bias, g, *, tq, tk, out_scale):
    bsz, seq, _ = k.shape
    nq = seq // tq
    assert tq == tk and nq >= 2
    return pl.pallas_call(
        functools.partial(_attn_kernel, tq=tq, tk=tk, out_scale=out_scale),
        grid=(bsz, ATT_HEADS),
        in_specs=[
            pl.BlockSpec(memory_space=pltpu.SMEM),
            pl.BlockSpec((1, nq, 1, 2 * ATT_QK_DIM, tq), lambda b, h: (b, 0, h, 0, 0)),
            pl.BlockSpec((1, seq, 2 * ATT_QK_DIM), lambda b, h: (b, 0, h)),
            pl.BlockSpec((1, seq // tk, ATT_V_ROWS, tk), lambda b, h: (b, 0, h, 0)),
            pl.BlockSpec((1, 2, tq // ATT_COLS, tk, ATT_COLS), lambda b, h: (h, 0, 0, 0, 0)),
            pl.BlockSpec((ATT_V_DIM, 1), lambda b, h: (0, 0)),
        ],
        out_specs=pl.BlockSpec((1, seq, ATT_V_DIM), lambda b, h: (b, 0, h)),
        out_shape=jax.ShapeDtypeStruct((bsz, seq, ATT_WIDTH), BF16),
        scratch_shapes=[
            pltpu.VMEM((2 * tq // ATT_COLS, 2 * ATT_QK_DIM, ATT_COLS), BF16),
            pltpu.VMEM((2 * tq // ATT_COLS, tk, ATT_COLS), F32),
            pltpu.VMEM((2 * tq // ATT_COLS, tk, ATT_COLS), F32),
            pltpu.VMEM((1, 2 * tq), F32),
            pltpu.VMEM((1, 2 * tq), F32),
            pltpu.VMEM((1, 2 * tq), F32),
            pltpu.VMEM((2 * tq // ATT_COLS, ATT_V_ROWS, ATT_COLS), F32),
        ],
        compiler_params=pltpu.CompilerParams(
            dimension_semantics=("arbitrary", "arbitrary"),
            vmem_limit_bytes=V7X_VMEM_LIMIT_BYTES),
        name="diffattn",
    )(lam, qT, k, vT, bias, g)


def _ssm_kernel(u_ref, wb_ref, wc_ref, lam_ref, d_ref, wglu_ref, bglu_ref, g_ref,
                o_ref, uin, res, xs, st, *, steps):
    rows = V7X_SUBLANES
    half = xs.shape[2] // 2
    lanes = V7X_LANES
    bsz = u_ref.shape[0]

    @pl.when(pl.program_id(0) == 0)
    def _():
        st[...] = jnp.zeros_like(st)

    for b in range(bsz):
        for c in range(SSM_LANE_GROUPS):
            uin[c, pl.ds(b, steps, stride=bsz), :] = u_ref[b, :, c * lanes:(c + 1) * lanes]

    ys = []
    xs[0] = _dot(uin[0].astype(BF16), wb_ref[0])
    for c in range(SSM_LANE_GROUPS):
        if c + 1 < SSM_LANE_GROUPS:
            xs[c + 1] = _dot(uin[c + 1].astype(BF16), wb_ref[c + 1])
        lre = lam_ref[c, :, 0:half]
        lim = lam_ref[c, :, half:2 * half]

        def step(t, carry, c=c, lre=lre, lim=lim):
            sre, sim = carry
            r = pl.multiple_of(t * rows, rows)
            bre = xs[c, pl.ds(r, rows), 0:half]
            bim = xs[c, pl.ds(r, rows), half:2 * half]
            nre = lre * sre - lim * sim + bre
            nim = lre * sim + lim * sre + bim
            xs[c, pl.ds(r, rows), 0:half] = nre
            xs[c, pl.ds(r, rows), half:2 * half] = nim
            return nre, nim

        sre, sim = lax.fori_loop(0, steps, step,
                                 (st[c, :, 0:half], st[c, :, half:2 * half]), unroll=True)
        st[c, :, 0:half] = sre
        st[c, :, half:2 * half] = sim
        ys.append(_dot(xs[c].astype(BF16), wc_ref[c]))

    u = jnp.concatenate([uin[c] for c in range(SSM_LANE_GROUPS)], axis=1)
    y = jnp.concatenate(ys, axis=1) + d_ref[...] * u
    gl = jax.nn.gelu(y)
    z = _dot(gl.astype(BF16), wglu_ref[...]) + bglu_ref[...]
    out = _rms(gl * jax.nn.sigmoid(z), g_ref[...])
    for c in range(SSM_LANE_GROUPS):
        res[c] = out[:, c * lanes:(c + 1) * lanes]
    for b in range(bsz):
        for c in range(SSM_LANE_GROUPS):
            o_ref[b, :, c * lanes:(c + 1) * lanes] = (
                res[c, pl.ds(b, steps, stride=bsz), :].astype(BF16))


def _ssm(u, wb, wc, lam, dskip, wglu, bglu, g, *, steps):
    bsz, seq, width = u.shape
    r = steps * bsz
    const = lambda shape: pl.BlockSpec(shape, lambda i: (0,) * len(shape),
                                       pipeline_mode=pl.Buffered(1))
    return pl.pallas_call(
        functools.partial(_ssm_kernel, steps=steps),
        grid=(seq // steps,),
        in_specs=[
            pl.BlockSpec((bsz, steps, width), lambda i: (0, i, 0)),
            const(wb.shape), const(wc.shape), const(lam.shape), const(dskip.shape),
            const(wglu.shape), const(bglu.shape), const(g.shape),
        ],
        out_specs=pl.BlockSpec((bsz, steps, width), lambda i: (0, i, 0)),
        out_shape=jax.ShapeDtypeStruct((bsz, seq, width), BF16),
        scratch_shapes=[
            pltpu.VMEM((SSM_LANE_GROUPS, r, V7X_LANES), F32),
            pltpu.VMEM((SSM_LANE_GROUPS, r, V7X_LANES), F32),
            pltpu.VMEM((SSM_LANE_GROUPS, r, wb.shape[2]), F32),
            pltpu.VMEM((SSM_LANE_GROUPS, V7X_SUBLANES, wb.shape[2]), F32),
        ],
        compiler_params=pltpu.CompilerParams(
            dimension_semantics=("arbitrary",),
            vmem_limit_bytes=V7X_VMEM_LIMIT_BYTES),
        name="s5ssm",
    )(u, wb, wc, lam, dskip, wglu, bglu, g)


def _ffn_kernel(x_ref, att_ref, ssm_ref, woa_ref, wos_ref, gf_ref, wg_ref, wu_ref, wd_ref,
                gl_ref, o_ref):
    x1 = x_ref[0] + _dot(att_ref[0], woa_ref[...]) + _dot(ssm_ref[0], wos_ref[...])
    h = _rms(x1, gf_ref[...]).astype(BF16)
    a = (jax.nn.silu(_dot(h, wg_ref[...])) * _dot(h, wu_ref[...])).astype(BF16)
    o_ref[0] = _rms(x1 + _dot(a, wd_ref[...]), gl_ref[...])


def _ffn(x, att, ssm, woa, wos, gf, wg, wu, wd, gl, *, tm):
    bsz, seq, d = x.shape
    const = lambda shape: pl.BlockSpec(shape, lambda b, s: (0,) * len(shape),
                                       pipeline_mode=pl.Buffered(1))
    return pl.pallas_call(
        _ffn_kernel,
        grid=(bsz, seq // tm),
        in_specs=[
            pl.BlockSpec((1, tm, d), lambda b, s: (b, s, 0)),
            pl.BlockSpec((1, tm, ATT_WIDTH), lambda b, s: (b, s, 0)),
            pl.BlockSpec((1, tm, SSM_WIDTH), lambda b, s: (b, s, 0)),
            const(woa.shape), const(wos.shape), const(gf.shape),
            const(wg.shape), const(wu.shape), const(wd.shape), const(gl.shape),
        ],
        out_specs=pl.BlockSpec((1, tm, d), lambda b, s: (b, s, 0)),
        out_shape=jax.ShapeDtypeStruct((bsz, seq, d), x.dtype),
        compiler_params=pltpu.CompilerParams(
            dimension_semantics=("arbitrary", "arbitrary"),
            vmem_limit_bytes=V7X_VMEM_LIMIT_BYTES),
        name="outproj_ffn",
    )(x, att, ssm, woa, wos, gf, wg, wu, wd, gl)


def _t5_bucket(n):
    max_exact = REL_BUCKETS // 2
    nf = jnp.maximum(n, 1).astype(F32)
    large = max_exact + (jnp.log(nf / max_exact) / math.log(REL_MAX_DIST / max_exact)
                         * (REL_BUCKETS - max_exact)).astype(jnp.int32)
    large = jnp.minimum(large, REL_BUCKETS - 1)
    return jnp.where(n < max_exact, n, large)


def _bias_tiles(rel_bias, tq, tk):
    assert tk >= REL_MAX_DIST
    ratio = tq // tk
    nd = tq + tk + 1
    neg = ratio * tk
    table = rel_bias.astype(F32)
    by_dist = table[_t5_bucket(jnp.arange(nd))]
    far = table[_t5_bucket(jnp.full((), REL_MAX_DIST, jnp.int32))]
    by_dist = (by_dist - far[None, :]) * LOG2E
    by_dist = jnp.concatenate([jnp.full((neg, ATT_HEADS), MASK_VALUE, F32), by_dist], axis=0).T
    m = tk + tq
    tiles = []
    for r in range(ratio + 1):
        start = neg - (tk - 1) - (r - 1) * tk
        c = by_dist[:, start:start + m]
        t = jnp.broadcast_to(c[:, None, :], (ATT_HEADS, tk, m)).reshape(ATT_HEADS, tk * m)
        t = t[:, :tk * (m - 1)].reshape(ATT_HEADS, tk, m - 1)
        tiles.append(t[:, :, tk - 1:tk - 1 + tq])
    tiles = jnp.stack(tiles, axis=1)
    tiles = tiles.reshape(ATT_HEADS, ratio + 1, tk, tq // ATT_COLS, ATT_COLS)
    return jnp.transpose(tiles, (0, 1, 3, 2, 4))


def _ssm_params(A_re, A_im, log_dt, B_re, B_im, C_re, C_im):
    lam = lax.complex(A_re.astype(F32), A_im.astype(F32))
    dt = jnp.exp(log_dt.astype(F32))[:, None]
    lam_bar = jnp.exp(lam * dt)
    b_bar = ((lam_bar - 1.0) / lam)[:, :, None] * lax.complex(B_re.astype(F32), B_im.astype(F32))
    nlg = SSM_LANE_GROUPS
    gpl = SSM_GROUPS // nlg
    eye = jnp.eye(gpl, dtype=F32)

    def in_blocks(w):
        w = w.reshape(nlg, gpl, SSM_STATE, SSM_GROUP)
        return jnp.einsum('kgnc,gh->kgchn', w, eye).reshape(nlg, gpl * SSM_GROUP, gpl * SSM_STATE)

    def out_blocks(w):
        w = w.reshape(nlg, gpl, SSM_GROUP, SSM_STATE)
        return jnp.einsum('kgcn,gh->kgnhc', w, eye).reshape(nlg, gpl * SSM_STATE, gpl * SSM_GROUP)

    wb = jnp.concatenate([in_blocks(b_bar.real), in_blocks(b_bar.imag)], axis=2).astype(BF16)
    wc = jnp.concatenate([out_blocks(C_re.astype(F32)), out_blocks(-C_im.astype(F32))],
                         axis=1).astype(BF16)
    lam_ri = jnp.concatenate([lam_bar.real.reshape(nlg, 1, -1), lam_bar.imag.reshape(nlg, 1, -1)],
                             axis=2)
    lam_ri = jnp.broadcast_to(lam_ri, (nlg, V7X_SUBLANES, lam_ri.shape[2]))
    return wb, wc, lam_ri


def kernel(x, norm_mix_g, w_in, lambda_q1, lambda_k1, lambda_q2, lambda_k2, subln_g, rel_bias,
           A_re, A_im, log_dt, B_re, B_im, C_re, C_im, D_skip, w_glu, b_glu, ssm_norm_g, w_out,
           norm_ffn_g, w_gate, w_up, w_down, norm_final_g):
    bsz, seq, d = x.shape
    assert d == D_MODEL and bsz == V7X_SUBLANES
    ts, tq, tk, tm = PROJ_ROWS, ATT_TQ, ATT_TK, FFN_ROWS
    assert seq % ts == 0 and seq % tq == 0 and seq % tm == 0 and seq % SSM_STEPS == 0
    assert ts % tk == 0 and ts % tq == 0 and tq == tk
    l = 0
    row = lambda v: v.astype(F32).reshape(1, -1)

    w = w_in[l].astype(F32)
    qk_scale = ATT_QK_DIM ** -0.5 * LOG2E
    wqT = (w[:, 0:QK_COLS] * qk_scale).T.astype(BF16)
    wk = w[:, QK_COLS:2 * QK_COLS].astype(BF16)
    wvT = w[:, 2 * QK_COLS:2 * QK_COLS + ATT_WIDTH].T.astype(BF16)
    wu = w[:, 2 * QK_COLS + ATT_WIDTH:].astype(BF16)
    lam_init = 0.8 - 0.6 * math.exp(-0.3 * l)
    lam = (jnp.exp(jnp.sum(lambda_q1[l].astype(F32) * lambda_k1[l].astype(F32)))
           - jnp.exp(jnp.sum(lambda_q2[l].astype(F32) * lambda_k2[l].astype(F32)))
           + lam_init).reshape(1, 1)
    bias = _bias_tiles(rel_bias, tq, tk)
    wb, wc, lam_ri = _ssm_params(A_re[l], A_im[l], log_dt[l], B_re[l], B_im[l], C_re[l], C_im[l])

    qT, k, vT, u = _inproj(x, row(norm_mix_g[l]), wqT, wk, wvT, wu, ts=ts, tq=tq, tk=tk)
    att = _attention(lam, qT, k, vT, bias, subln_g[l].astype(F32).reshape(-1, 1),
                     tq=tq, tk=tk, out_scale=1.0 - lam_init)
    ssm = _ssm(u, wb, wc, lam_ri, row(D_skip[l]),
               w_glu[l].astype(BF16), row(b_glu[l]), row(ssm_norm_g[l]), steps=SSM_STEPS)

    wo = w_out[l].astype(BF16)
    return _ffn(x, att, ssm, wo[0:ATT_WIDTH], wo[ATT_WIDTH:],
                row(norm_ffn_g[l]), w_gate[l].astype(BF16), w_up[l].astype(BF16),
                w_down[l].astype(BF16), row(norm_final_g), tm=tm)
```

```python
import functools
import math

import jax
import jax.numpy as jnp
from jax import lax
from jax.experimental import pallas as pl
from jax.experimental.pallas import tpu as pltpu

F32 = jnp.float32
BF16 = jnp.bfloat16

D_MODEL = 1024
ATT_WIDTH = 512
SSM_WIDTH = 512
ATT_HEADS = 4
ATT_QK_DIM = 64
ATT_V_DIM = 128
ATT_V_ROWS = ATT_V_DIM + 16
QK_COLS = 512
SSM_GROUP = 16
SSM_GROUPS = 32
SSM_STATE = 64
REL_BUCKETS = 32
REL_MAX_DIST = 128
EPS = 1e-6
LOG2E = math.log2(math.e)
MASK_VALUE = -1e30

V7X_SUBLANES = 8
V7X_LANES = 128
V7X_VMEM_LIMIT_BYTES = 56 * 1024 * 1024

PROJ_ROWS = 1024
ATT_TQ = 512
ATT_TK = 512
ATT_COLS = 256
ATT_UNROLL = 8
SSM_STEPS = 64
SSM_LANE_GROUPS = 4
FFN_ROWS = 512

_NT = (((1,), (1,)), ((), ()))


def _dot(a, b):
    return jnp.dot(a, b, preferred_element_type=F32)


def _rms(x, g):
    ms = jnp.mean(x * x, axis=-1, keepdims=True)
    return x * lax.rsqrt(ms + EPS) * g


def _inproj_kernel(x_ref, g_ref, wqT_ref, wk_ref, wvT_ref, wu_ref,
                   qT_ref, k_ref, vT_ref, u_ref, *, tq, tk):
    h = _rms(x_ref[0], g_ref[...]).astype(BF16)
    qT = lax.dot_general(wqT_ref[...], h, _NT, preferred_element_type=F32)
    qT = qT.astype(BF16)
    for c in range(qT_ref.shape[1]):
        for hd in range(ATT_HEADS):
            qT_ref[0, c, hd] = qT[hd * 2 * ATT_QK_DIM:(hd + 1) * 2 * ATT_QK_DIM,
                                  c * tq:(c + 1) * tq]
    k_ref[0] = _dot(h, wk_ref[...]).astype(BF16)
    vT = lax.dot_general(wvT_ref[...], h, _NT, preferred_element_type=F32).astype(BF16)
    ones = jnp.ones((ATT_V_ROWS - ATT_V_DIM, tk), BF16)
    for c in range(vT_ref.shape[1]):
        for hd in range(ATT_HEADS):
            r0 = hd * ATT_V_ROWS
            vT_ref[0, c, r0:r0 + ATT_V_DIM, :] = vT[hd * ATT_V_DIM:(hd + 1) * ATT_V_DIM,
                                                    c * tk:(c + 1) * tk]
            vT_ref[0, c, r0 + ATT_V_DIM:r0 + ATT_V_ROWS, :] = ones
    u_ref[0] = _dot(h, wu_ref[...])


def _inproj(x, g, wqT, wk, wvT, wu, *, ts, tq, tk):
    bsz, seq, d = x.shape
    ns = seq // ts
    const = lambda shape: pl.BlockSpec(shape, lambda b, s: (0,) * len(shape),
                                       pipeline_mode=pl.Buffered(1))
    return pl.pallas_call(
        functools.partial(_inproj_kernel, tq=tq, tk=tk),
        grid=(bsz, ns),
        in_specs=[
            pl.BlockSpec((1, ts, d), lambda b, s: (b, s, 0)),
            const((1, d)),
            const((QK_COLS, d)),
            const((d, QK_COLS)),
            const((ATT_WIDTH, d)),
            const((d, SSM_WIDTH)),
        ],
        out_specs=[
            pl.BlockSpec((1, ts // tq, ATT_HEADS, 2 * ATT_QK_DIM, tq),
                         lambda b, s: (b, s, 0, 0, 0)),
            pl.BlockSpec((1, ts, QK_COLS), lambda b, s: (b, s, 0)),
            pl.BlockSpec((1, ts // tk, ATT_HEADS * ATT_V_ROWS, tk), lambda b, s: (b, s, 0, 0)),
            pl.BlockSpec((1, ts, SSM_WIDTH), lambda b, s: (b, s, 0)),
        ],
        out_shape=[
            jax.ShapeDtypeStruct((bsz, seq // tq, ATT_HEADS, 2 * ATT_QK_DIM, tq), BF16),
            jax.ShapeDtypeStruct((bsz, seq, QK_COLS), BF16),
            jax.ShapeDtypeStruct((bsz, seq // tk, ATT_HEADS * ATT_V_ROWS, tk), BF16),
            jax.ShapeDtypeStruct((bsz, seq, SSM_WIDTH), F32),
        ],
        compiler_params=pltpu.CompilerParams(
            dimension_semantics=("arbitrary", "arbitrary"),
            vmem_limit_bytes=V7X_VMEM_LIMIT_BYTES),
        name="inproj",
    )(x, g, wqT, wk, wvT, wu)


def _attn_kernel(lam_ref, qT_ref, k_ref, vT_ref, bias_ref, g_ref, o_ref,
                 qbd, s_a, s_b, mx_a, mx_b, m_s, acc_s, *, tq, tk, out_scale):
    assert tq == tk
    nq = qT_ref.shape[1]
    dk = ATT_QK_DIM
    a, b = bufs = ((s_a, mx_a), (s_b, mx_b))
    nblk = 2 * tq // ATT_COLS

    def q_load(qi):
        for c0 in range(0, tq, ATT_COLS):
            qbd[c0 // ATT_COLS, 0:dk, :] = qT_ref[0, qi, 0, 0:dk, c0:c0 + ATT_COLS]
            qbd[(tq + c0) // ATT_COLS, dk:2 * dk, :] = qT_ref[0, qi, 0, dk:2 * dk,
                                                              c0:c0 + ATT_COLS]

    def q_reset():
        m_s[...] = jnp.full_like(m_s, -jnp.inf)
        acc_s[...] = jnp.zeros_like(acc_s)

    def q_init(qi):
        q_load(qi)
        q_reset()

    def finalize(qi):
        lam = lam_ref[0, 0]
        acc = jnp.concatenate([acc_s[cb, 0:ATT_V_DIM, :] for cb in range(nblk)], axis=1)
        l = jnp.concatenate([acc_s[cb, ATT_V_DIM:ATT_V_DIM + 1, :]
                             for cb in range(nblk)], axis=1)
        inv = 1.0 / l
        o = acc[:, 0:tq] * inv[:, 0:tq] - lam * (acc[:, tq:2 * tq] * inv[:, tq:2 * tq])
        ms = jnp.mean(o * o, axis=0, keepdims=True)
        o = o * lax.rsqrt(ms + EPS) * g_ref[...] * out_scale
        o_ref[0, pl.ds(pl.multiple_of(qi * tq, tq), tq), :] = o.T.astype(BF16)

    def live_rows(r, c0):
        if r is None or r == 0:
            return tk
        return max(0, min(tk, c0 % tq + ATT_COLS - (r - 1) * tk))

    def logits(tile, bufs2):
        j, r = tile
        buf, mx = bufs2
        r0 = pl.multiple_of(j * tk, tk)
        for c0 in range(0, 2 * tq, ATT_COLS):
            rows = live_rows(r, c0)
            if rows:
                s = _dot(k_ref[0, pl.ds(r0, rows), :], qbd[c0 // ATT_COLS])
                if r is not None:
                    s = s + bias_ref[0, r, (c0 % tq) // ATT_COLS, 0:rows, :]
                buf[c0 // ATT_COLS, 0:rows, :] = s
                mx[:, c0:c0 + ATT_COLS] = jnp.max(s, axis=0, keepdims=True)

    def softmax_pv(tile, bufs2):
        j, r = tile
        buf, mx = bufs2
        for c0 in range(0, 2 * tq, ATT_COLS):
            rows = live_rows(r, c0)
            if rows == 0:
                continue
            cols = slice(c0, c0 + ATT_COLS)
            m_old = m_s[:, cols]
            m_new = jnp.maximum(m_old, mx[:, cols])
            alpha = jnp.exp2(m_old - m_new)
            p = jnp.exp2(buf[c0 // ATT_COLS, 0:rows, :] - m_new)
            cb = c0 // ATT_COLS
            acc_s[cb] = alpha * acc_s[cb] + _dot(vT_ref[0, j, :, 0:rows], p.astype(BF16))
            m_s[:, cols] = m_new

    def run_tiles(tiles, preloaded):
        if not preloaded:
            logits(tiles[0], bufs[0])
        for i, tile in enumerate(tiles):
            if i + 1 < len(tiles):
                logits(tiles[i + 1], bufs[(i + 1) % 2])
            softmax_pv(tile, bufs[i % 2])

    qbd[...] = jnp.zeros_like(qbd)
    q_init(0)
    run_tiles([(0, 1)], preloaded=False)
    finalize(0)
    q_init(1)
    run_tiles([(0, 0), (1, 1)], preloaded=False)

    def q_block(qi, carry):
        q_load(qi)
        logits((0, None), a)
        finalize(qi - 1)
        q_reset()

        un = ATT_UNROLL

        def trip(g, c):
            for i in range(un):
                logits((un * g + i + 1, None), bufs[(i + 1) % 2])
                softmax_pv((un * g + i, None), bufs[i % 2])
            return c

        ntrips = (qi - 2) // un
        lax.fori_loop(0, ntrips, trip, 0)
        t = un * ntrips

        for rem in range(2, un + 2):
            @pl.when(qi - t == rem)
            def _(rem=rem):
                far = [(t + i, None) for i in range(rem - 1)]
                run_tiles(far + [(t + rem - 1, 0), (qi, 1)], preloaded=True)

        return carry

    lax.fori_loop(2, nq, q_block, 0)
    finalize(nq - 1)


def _attention(lam, qT, k, vT, bias, g, *, tq, tk, out_scale):
    bsz, seq, _ = k.shape
    nq = seq // tq
    assert tq == tk and nq >= 2
    return pl.pallas_call(
        functools.partial(_attn_kernel, tq=tq, tk=tk, out_scale=out_scale),
        grid=(bsz, ATT_HEADS),
        in_specs=[
            pl.BlockSpec(memory_space=pltpu.SMEM),
            pl.BlockSpec((1, nq, 1, 2 * ATT_QK_DIM, tq), lambda b, h: (b, 0, h, 0, 0)),
            pl.BlockSpec((1, seq, 2 * ATT_QK_DIM), lambda b, h: (b, 0, h)),
            pl.BlockSpec((1, seq // tk, ATT_V_ROWS, tk), lambda b, h: (b, 0, h, 0)),
            pl.BlockSpec((1, 2, tq // ATT_COLS, tk, ATT_COLS), lambda b, h: (h, 0, 0, 0, 0)),
            pl.BlockSpec((ATT_V_DIM, 1), lambda b, h: (0, 0)),
        ],
        out_specs=pl.BlockSpec((1, seq, ATT_V_DIM), lambda b, h: (b, 0, h)),
        out_shape=jax.ShapeDtypeStruct((bsz, seq, ATT_WIDTH), BF16),
        scratch_shapes=[
            pltpu.VMEM((2 * tq // ATT_COLS, 2 * ATT_QK_DIM, ATT_COLS), BF16),
            pltpu.VMEM((2 * tq // ATT_COLS, tk, ATT_COLS), F32),
            pltpu.VMEM((2 * tq // ATT_COLS, tk, ATT_COLS), F32),
            pltpu.VMEM((1, 2 * tq), F32),
            pltpu.VMEM((1, 2 * tq), F32),
            pltpu.VMEM((1, 2 * tq), F32),
            pltpu.VMEM((2 * tq // ATT_COLS, ATT_V_ROWS, ATT_COLS), F32),
        ],
        compiler_params=pltpu.CompilerParams(
            dimension_semantics=("arbitrary", "arbitrary"),
            vmem_limit_bytes=V7X_VMEM_LIMIT_BYTES),
        name="diffattn",
    )(lam, qT, k, vT, bias, g)


def _ssm_kernel(u_ref, wb_ref, wc_ref, lam_ref, d_ref, wglu_ref, bglu_ref, g_ref,
                o_ref, uin, res, xs, st, *, steps):
    rows = V7X_SUBLANES
    half = xs.shape[2] // 2
    lanes = V7X_LANES
    bsz = u_ref.shape[0]

    @pl.when(pl.program_id(0) == 0)
    def _():
        st[...] = jnp.zeros_like(st)

    for b in range(bsz):
        for c in range(SSM_LANE_GROUPS):
            uin[c, pl.ds(b, steps, stride=bsz), :] = u_ref[b, :, c * lanes:(c + 1) * lanes]

    ys = []
    xs[0] = _dot(uin[0].astype(BF16), wb_ref[0])
    for c in range(SSM_LANE_GROUPS):
        if c + 1 < SSM_LANE_GROUPS:
            xs[c + 1] = _dot(uin[c + 1].astype(BF16), wb_ref[c + 1])
        lre = lam_ref[c, :, 0:half]
        lim = lam_ref[c, :, half:2 * half]

        def step(t, carry, c=c, lre=lre, lim=lim):
            sre, sim = carry
            r = pl.multiple_of(t * rows, rows)
            bre = xs[c, pl.ds(r, rows), 0:half]
            bim = xs[c, pl.ds(r, rows), half:2 * half]
            nre = lre * sre - lim * sim + bre
            nim = lre * sim + lim * sre + bim
            xs[c, pl.ds(r, rows), 0:half] = nre
            xs[c, pl.ds(r, rows), half:2 * half] = nim
            return nre, nim

        sre, sim = lax.fori_loop(0, steps, step,
                                 (st[c, :, 0:half], st[c, :, half:2 * half]), unroll=True)
        st[c, :, 0:half] = sre
        st[c, :, half:2 * half] = sim
        ys.append(_dot(xs[c].astype(BF16), wc_ref[c]))

    u = jnp.concatenate([uin[c] for c in range(SSM_LANE_GROUPS)], axis=1)
    y = jnp.concatenate(ys, axis=1) + d_ref[...] * u
    gl = jax.nn.gelu(y)
    z = _dot(gl.astype(BF16), wglu_ref[...]) + bglu_ref[...]
    out = _rms(gl * jax.nn.sigmoid(z), g_ref[...])
    for c in range(SSM_LANE_GROUPS):
        res[c] = out[:, c * lanes:(c + 1) * lanes]
    for b in range(bsz):
        for c in range(SSM_LANE_GROUPS):
            o_ref[b, :, c * lanes:(c + 1) * lanes] = (
                res[c, pl.ds(b, steps, stride=bsz), :].astype(BF16))


def _ssm(u, wb, wc, lam, dskip, wglu, bglu, g, *, steps):
    bsz, seq, width = u.shape
    r = steps * bsz
    const = lambda shape: pl.BlockSpec(shape, lambda i: (0,) * len(shape),
                                       pipeline_mode=pl.Buffered(1))
    return pl.pallas_call(
        functools.partial(_ssm_kernel, steps=steps),
        grid=(seq // steps,),
        in_specs=[
            pl.BlockSpec((bsz, steps, width), lambda i: (0, i, 0)),
            const(wb.shape), const(wc.shape), const(lam.shape), const(dskip.shape),
            const(wglu.shape), const(bglu.shape), const(g.shape),
        ],
        out_specs=pl.BlockSpec((bsz, steps, width), lambda i: (0, i, 0)),
        out_shape=jax.ShapeDtypeStruct((bsz, seq, width), BF16),
        scratch_shapes=[
            pltpu.VMEM((SSM_LANE_GROUPS, r, V7X_LANES), F32),
            pltpu.VMEM((SSM_LANE_GROUPS, r, V7X_LANES), F32),
            pltpu.VMEM((SSM_LANE_GROUPS, r, wb.shape[2]), F32),
            pltpu.VMEM((SSM_LANE_GROUPS, V7X_SUBLANES, wb.shape[2]), F32),
        ],
        compiler_params=pltpu.CompilerParams(
            dimension_semantics=("arbitrary",),
            vmem_limit_bytes=V7X_VMEM_LIMIT_BYTES),
        name="s5ssm",
    )(u, wb, wc, lam, dskip, wglu, bglu, g)


def _ffn_kernel(x_ref, att_ref, ssm_ref, woa_ref, wos_ref, gf_ref, wg_ref, wu_ref, wd_ref,
                gl_ref, o_ref):
    x1 = x_ref[0] + _dot(att_ref[0], woa_ref[...]) + _dot(ssm_ref[0], wos_ref[...])
    h = _rms(x1, gf_ref[...]).astype(BF16)
    a = (jax.nn.silu(_dot(h, wg_ref[...])) * _dot(h, wu_ref[...])).astype(BF16)
    o_ref[0] = _rms(x1 + _dot(a, wd_ref[...]), gl_ref[...])


def _ffn(x, att, ssm, woa, wos, gf, wg, wu, wd, gl, *, tm):
    bsz, seq, d = x.shape
    const = lambda shape: pl.BlockSpec(shape, lambda b, s: (0,) * len(shape),
                                       pipeline_mode=pl.Buffered(1))
    return pl.pallas_call(
        _ffn_kernel,
        grid=(bsz, seq // tm),
        in_specs=[
            pl.BlockSpec((1, tm, d), lambda b, s: (b, s, 0)),
            pl.BlockSpec((1, tm, ATT_WIDTH), lambda b, s: (b, s, 0)),
            pl.BlockSpec((1, tm, SSM_WIDTH), lambda b, s: (b, s, 0)),
            const(woa.shape), const(wos.shape), const(gf.shape),
            const(wg.shape), const(wu.shape), const(wd.shape), const(gl.shape),
        ],
        out_specs=pl.BlockSpec((1, tm, d), lambda b, s: (b, s, 0)),
        out_shape=jax.ShapeDtypeStruct((bsz, seq, d), x.dtype),
        compiler_params=pltpu.CompilerParams(
            dimension_semantics=("arbitrary", "arbitrary"),
            vmem_limit_bytes=V7X_VMEM_LIMIT_BYTES),
        name="outproj_ffn",
    )(x, att, ssm, woa, wos, gf, wg, wu, wd, gl)


def _t5_bucket(n):
    max_exact = REL_BUCKETS // 2
    nf = jnp.maximum(n, 1).astype(F32)
    large = max_exact + (jnp.log(nf / max_exact) / math.log(REL_MAX_DIST / max_exact)
                         * (REL_BUCKETS - max_exact)).astype(jnp.int32)
    large = jnp.minimum(large, REL_BUCKETS - 1)
    return jnp.where(n < max_exact, n, large)


def _bias_tiles(rel_bias, tq, tk):
    assert tk >= REL_MAX_DIST
    ratio = tq // tk
    nd = tq + tk + 1
    neg = ratio * tk
    table = rel_bias.astype(F32)
    by_dist = table[_t5_bucket(jnp.arange(nd))]
    far = table[_t5_bucket(jnp.full((), REL_MAX_DIST, jnp.int32))]
    by_dist = (by_dist - far[None, :]) * LOG2E
    by_dist = jnp.concatenate([jnp.full((neg, ATT_HEADS), MASK_VALUE, F32), by_dist], axis=0).T
    m = tk + tq
    tiles = []
    for r in range(ratio + 1):
        start = neg - (tk - 1) - (r - 1) * tk
        c = by_dist[:, start:start + m]
        t = jnp.broadcast_to(c[:, None, :], (ATT_HEADS, tk, m)).reshape(ATT_HEADS, tk * m)
        t = t[:, :tk * (m - 1)].reshape(ATT_HEADS, tk, m - 1)
        tiles.append(t[:, :, tk - 1:tk - 1 + tq])
    tiles = jnp.stack(tiles, axis=1)
    tiles = tiles.reshape(ATT_HEADS, ratio + 1, tk, tq // ATT_COLS, ATT_COLS)
    return jnp.transpose(tiles, (0, 1, 3, 2, 4))


def _ssm_params(A_re, A_im, log_dt, B_re, B_im, C_re, C_im):
    lam = lax.complex(A_re.astype(F32), A_im.astype(F32))
    dt = jnp.exp(log_dt.astype(F32))[:, None]
    lam_bar = jnp.exp(lam * dt)
    b_bar = ((lam_bar - 1.0) / lam)[:, :, None] * lax.complex(B_re.astype(F32), B_im.astype(F32))
    nlg = SSM_LANE_GROUPS
    gpl = SSM_GROUPS // nlg
    eye = jnp.eye(gpl, dtype=F32)

    def in_blocks(w):
        w = w.reshape(nlg, gpl, SSM_STATE, SSM_GROUP)
        return jnp.einsum('kgnc,gh->kgchn', w, eye).reshape(nlg, gpl * SSM_GROUP, gpl * SSM_STATE)

    def out_blocks(w):
        w = w.reshape(nlg, gpl, SSM_GROUP, SSM_STATE)
        return jnp.einsum('kgcn,gh->kgnhc', w, eye).reshape(nlg, gpl * SSM_STATE, gpl * SSM_GROUP)

    wb = jnp.concatenate([in_blocks(b_bar.real), in_blocks(b_bar.imag)], axis=2).astype(BF16)
    wc = jnp.concatenate([out_blocks(C_re.astype(F32)), out_blocks(-C_im.astype(F32))],
                         axis=1).astype(BF16)
    lam_ri = jnp.concatenate([lam_bar.real.reshape(nlg, 1, -1), lam_bar.imag.reshape(nlg, 1, -1)],
                             axis=2)
    lam_ri = jnp.broadcast_to(lam_ri, (nlg, V7X_SUBLANES, lam_ri.shape[2]))
    return wb, wc, lam_ri


def kernel(x, norm_mix_g, w_in, lambda_q1, lambda_k1, lambda_q2, lambda_k2, subln_g, rel_bias,
           A_re, A_im, log_dt, B_re, B_im, C_re, C_im, D_skip, w_glu, b_glu, ssm_norm_g, w_out,
           norm_ffn_g, w_gate, w_up, w_down, norm_final_g):
    bsz, seq, d = x.shape
    assert d == D_MODEL and bsz == V7X_SUBLANES
    ts, tq, tk, tm = PROJ_ROWS, ATT_TQ, ATT_TK, FFN_ROWS
    assert seq % ts == 0 and seq % tq == 0 and seq % tm == 0 and seq % SSM_STEPS == 0
    assert ts % tk == 0 and ts % tq == 0 and tq == tk
    l = 0
    row = lambda v: v.astype(F32).reshape(1, -1)

    w = w_in[l].astype(F32)
    qk_scale = ATT_QK_DIM ** -0.5 * LOG2E
    wqT = (w[:, 0:QK_COLS] * qk_scale).T.astype(BF16)
    wk = w[:, QK_COLS:2 * QK_COLS].astype(BF16)
    wvT = w[:, 2 * QK_COLS:2 * QK_COLS + ATT_WIDTH].T.astype(BF16)
    wu = w[:, 2 * QK_COLS + ATT_WIDTH:].astype(BF16)
    lam_init = 0.8 - 0.6 * math.exp(-0.3 * l)
    lam = (jnp.exp(jnp.sum(lambda_q1[l].astype(F32) * lambda_k1[l].astype(F32)))
           - jnp.exp(jnp.sum(lambda_q2[l].astype(F32) * lambda_k2[l].astype(F32)))
           + lam_init).reshape(1, 1)
    bias = _bias_tiles(rel_bias, tq, tk)
    wb, wc, lam_ri = _ssm_params(A_re[l], A_im[l], log_dt[l], B_re[l], B_im[l], C_re[l], C_im[l])

    qT, k, vT, u = _inproj(x, row(norm_mix_g[l]), wqT, wk, wvT, wu, ts=ts, tq=tq, tk=tk)
    att = _attention(lam, qT, k, vT, bias, subln_g[l].astype(F32).reshape(-1, 1),
                     tq=tq, tk=tk, out_scale=1.0 - lam_init)
    ssm = _ssm(u, wb, wc, lam_ri, row(D_skip[l]),
               w_glu[l].astype(BF16), row(b_glu[l]), row(ssm_norm_g[l]), steps=SSM_STEPS)

    wo = w_out[l].astype(BF16)
    return _ffn(x, att, ssm, wo[0:ATT_WIDTH], wo[ATT_WIDTH:],
                row(norm_ffn_g[l]), w_gate[l].astype(BF16), w_up[l].astype(BF16),
                w_down[l].astype(BF16), row(norm_final_g), tm=tm)
```
